```python
import jax, jax.numpy as jnp
from jax import lax
import numpy as np

D_MODEL = 4096
BATCH = 2
SEQ = 4096
DEPTH = 1
DEC_BATCH = 32
DEC_SEQ = 1
PAST_LEN = 8192
PAGE_SIZE = 128

HEAD_DIM = 128
N_HEADS = D_MODEL // HEAD_DIM
H_NSA = N_HEADS // 2
H_DSA = N_HEADS - H_NSA
KV_NSA = 2
KV_DSA = 4
GRP_NSA = H_NSA // KV_NSA
GRP_DSA = H_DSA // KV_DSA
CMP_LEN = 32
CMP_STRIDE = 16
CMP_HID = 256
SEL_LEN = 64
SEL_TOP = 16
WINDOW = 512
IDX_HEADS = 16
IDX_DIM = 64
DSA_TOPK_MAX = 256
D_FF = 4 * D_MODEL
PLE_DIM = 256
ROPE_THETA = 10000.0
NORM_EPS = 1e-6
Q_BLOCK = 128
NEG_INF = -1e30
FORCE_BONUS = 1e4
SPLITS = (H_NSA * HEAD_DIM, KV_NSA * 2 * HEAD_DIM, KV_NSA * 2 * HEAD_DIM, KV_NSA * 2 * HEAD_DIM, 3 * H_NSA,
          H_DSA * HEAD_DIM, KV_DSA * 2 * HEAD_DIM, IDX_HEADS * IDX_DIM, IDX_DIM, IDX_HEADS, 2 * D_MODEL)
PROJ_W = sum(SPLITS)

kernel_name = 'nsa_dsa_gated_hybrid_step'


def rmsnorm(x, g):
    xf = x.astype(jnp.float32)
    y = xf * lax.rsqrt(jnp.mean(xf * xf, axis=-1, keepdims=True) + NORM_EPS)
    return (y * g.astype(jnp.float32)).astype(x.dtype)


def rope(x, pos):
    half = x.shape[-1] // 2
    inv = ROPE_THETA ** (-jnp.arange(half, dtype=jnp.float32) / half)
    ang = pos.astype(jnp.float32)[:, None] * inv[None, :]
    ang = ang.reshape(ang.shape[:1] + (1,) * (x.ndim - 3) + (half,))
    cos, sin = jnp.cos(ang), jnp.sin(ang)
    xf = x.astype(jnp.float32)
    x1, x2 = xf[..., :half], xf[..., half:]
    return jnp.concatenate([x1 * cos - x2 * sin, x1 * sin + x2 * cos], axis=-1).astype(x.dtype)


def rope_kv(kv, pos):
    return jnp.stack([rope(kv[..., 0, :], pos), kv[..., 1, :]], axis=-2)


def masked_softmax(s, mask):
    p = jax.nn.softmax(jnp.where(mask, s, NEG_INF), axis=-1)
    return jnp.where(mask, p, 0.0)


def gather_rows(src, tok):
    return jax.vmap(lambda s, t: s[t])(src, tok)


def gather_rows_heads(src, tok):
    per_head = jax.vmap(lambda s, t: s[t], in_axes=(1, 1), out_axes=1)
    return jax.vmap(per_head)(src, tok)


def _paged_rows_one(pool, pt, new, tok):
    past_len = pt.shape[0] * PAGE_SIZE
    tp = jnp.minimum(tok, past_len - 1)
    rows_past = pool[pt[tp // PAGE_SIZE], tp % PAGE_SIZE]
    rows_new = new[jnp.clip(tok - past_len, 0, new.shape[0] - 1)].astype(rows_past.dtype)
    is_past = (tok < past_len).reshape(tok.shape + (1,) * (rows_past.ndim - tok.ndim))
    return jnp.where(is_past, rows_past, rows_new)


def paged_rows(pool, page_table, new, tok):
    return jax.vmap(_paged_rows_one, in_axes=(None, 0, 0, 0))(pool, page_table, new, tok)


def paged_rows_heads(pool, page_table, new, tok):
    per_head = jax.vmap(_paged_rows_one, in_axes=(2, None, 1, 1), out_axes=1)
    return jax.vmap(per_head, in_axes=(None, 0, 0, 0))(pool, page_table, new, tok)


def logical_rows(pool, page_table, new):
    past = pool[page_table]
    past = past.reshape((past.shape[0], past.shape[1] * past.shape[2]) + past.shape[3:])
    return jnp.concatenate([past, new.astype(past.dtype)], axis=1)


def project(h, w_in, pos):
    B, S, _ = h.shape
    z = jnp.einsum('bsd,dp->bsp', h, w_in)
    offs = np.cumsum(SPLITS)[:-1].tolist()
    q_n, kv_c, kv_s, kv_w, g_n, q_d, kv_d, qi, ki, wi, g_m = jnp.split(z, offs, axis=-1)
    q_n = rope(q_n.reshape(B, S, KV_NSA, GRP_NSA, HEAD_DIM), pos)
    kv_c = rope_kv(kv_c.reshape(B, S, KV_NSA, 2, HEAD_DIM), pos)
    kv_s = rope_kv(kv_s.reshape(B, S, KV_NSA, 2, HEAD_DIM), pos)
    kv_w = rope_kv(kv_w.reshape(B, S, KV_NSA, 2, HEAD_DIM), pos)
    g_n = jax.nn.sigmoid(g_n.astype(jnp.float32)).astype(h.dtype).reshape(B, S, KV_NSA, GRP_NSA, 3)
    q_d = rope(q_d.reshape(B, S, KV_DSA, GRP_DSA, HEAD_DIM), pos)
    kv_d = rope_kv(kv_d.reshape(B, S, KV_DSA, 2, HEAD_DIM), pos)
    qi = rope(qi.reshape(B, S, IDX_HEADS, IDX_DIM), pos)
    ki = rope(ki, pos)
    g_m = jax.nn.sigmoid(g_m.astype(jnp.float32)).astype(h.dtype).reshape(B, S, 2, D_MODEL)
    return q_n, kv_c, kv_s, kv_w, g_n, q_d, kv_d, qi, ki, wi, g_m


def compress(kv, pos_emb, w1, w2):
    L = kv.shape[1]
    nc = (L - CMP_LEN) // CMP_STRIDE + 1
    idx = jnp.arange(nc)[:, None] * CMP_STRIDE + jnp.arange(CMP_LEN)[None, :]
    blocks = kv[:, idx] + jnp.transpose(pos_emb, (1, 0, 2))[:, None].astype(kv.dtype)
    hid = jax.nn.relu(jnp.einsum('bnlgcd,cldh->bngch', blocks, w1))
    return jnp.einsum('bngch,chd->bngcd', hid, w2)


def nsa_core(q, qpos, kvc, gather_sel, kvw, wpos, g, n_sel, L):
    B, Q = q.shape[0], q.shape[1]
    scale = HEAD_DIM ** -0.5
    kc, vc = kvc[..., 0, :], kvc[..., 1, :]
    blk = jnp.arange(kvc.shape[1])
    s_c = jnp.einsum('bqgrd,bngd->bqgrn', q, kc).astype(jnp.float32) * scale
    vis_c = (blk * CMP_STRIDE + CMP_LEN - 1)[None, :] <= qpos[:, None]
    p_c = masked_softmax(s_c, vis_c[None, :, None, None, :])
    o_c = jnp.einsum('bqgrn,bngd->bqgrd', p_c.astype(vc.dtype), vc)
    sb = jnp.arange(n_sel)
    overlap = ((blk[:, None] * CMP_STRIDE < (sb[None, :] + 1) * SEL_LEN)
               & (blk[:, None] * CMP_STRIDE + CMP_LEN > sb[None, :] * SEL_LEN)).astype(jnp.float32)
    imp = jnp.einsum('bqgrn,nj->bqgj', p_c, overlap)
    cur = qpos // SEL_LEN
    vis_s = sb[None, :] <= cur[:, None]
    forced = (sb[None, :] == 0) | (sb[None, :] == cur[:, None]) | (sb[None, :] == cur[:, None] - 1)
    score = jnp.where(vis_s[None, :, None, :], imp + jnp.where(forced, FORCE_BONUS, 0.0)[None, :, None, :], NEG_INF)
    n_top = min(SEL_TOP, n_sel)
    _, top_blk = lax.top_k(score, n_top)
    tok = (top_blk[..., None] * SEL_LEN + jnp.arange(SEL_LEN)).reshape(top_blk.shape[:3] + (n_top * SEL_LEN,))
    valid = tok <= qpos[None, :, None, None]
    kvs = gather_sel(jnp.minimum(tok, L - 1))
    ks, vs = kvs[..., 0, :], kvs[..., 1, :]
    s_s = jnp.einsum('bqgrd,bqgtd->bqgrt', q, ks).astype(jnp.float32) * scale
    p_s = masked_softmax(s_s, valid[:, :, :, None, :])
    o_s = jnp.einsum('bqgrt,bqgtd->bqgrd', p_s.astype(vs.dtype), vs)
    kw, vw = kvw[..., 0, :], kvw[..., 1, :]
    s_w = jnp.einsum('bqgrd,bkgd->bqgrk', q, kw).astype(jnp.float32) * scale
    dpos = qpos[:, None] - wpos[None, :]
    vis_w = (dpos >= 0) & (dpos < WINDOW) & (wpos[None, :] >= 0)
    p_w = masked_softmax(s_w, vis_w[None, :, None, None, :])
    o_w = jnp.einsum('bqgrk,bkgd->bqgrd', p_w.astype(vw.dtype), vw)
    o = g[..., 0:1] * o_c + g[..., 1:2] * o_s + g[..., 2:3] * o_w
    return o.reshape(B, Q, H_NSA * HEAD_DIM)


def dsa_core(q, qpos, qi, wi, ki, gather_kv):
    B, Q = q.shape[0], q.shape[1]
    L = ki.shape[1]
    topk = min(DSA_TOPK_MAX, L // 4)
    sc = jnp.einsum('bqhd,bsd->bqhs', qi, ki).astype(jnp.float32) * (IDX_DIM ** -0.5)
    w = wi.astype(jnp.float32) * (IDX_HEADS ** -0.5)
    idx_score = jnp.einsum('bqhs,bqh->bqs', jax.nn.relu(sc), w)
    vis = jnp.arange(L)[None, :] <= qpos[:, None]
    idx_score = jnp.where(vis[None], idx_score, NEG_INF)
    _, sel = lax.top_k(idx_score, topk)
    valid = sel <= qpos[None, :, None]
    kv = gather_kv(sel)
    k, v = kv[..., 0, :], kv[..., 1, :]
    s = jnp.einsum('bqgrd,bqkgd->bqgrk', q, k).astype(jnp.float32) * (HEAD_DIM ** -0.5)
    p = masked_softmax(s, valid[:, :, None, None, :])
    o = jnp.einsum('bqgrk,bqkgd->bqgrd', p.astype(v.dtype), v)
    return o.reshape(B, Q, H_DSA * HEAD_DIM)


def prompt_mixer(h, w_in_l, cmp_pos_l, cmp_w1_l, cmp_w2_l):
    B, S, _ = h.shape
    pos = jnp.arange(S)
    q_n, kv_c, kv_s, kv_w, g_n, q_d, kv_d, qi, ki, wi, g_m = project(h, w_in_l, pos)
    kvc = compress(kv_c, cmp_pos_l, cmp_w1_l, cmp_w2_l)
    kv_w_pad = jnp.pad(kv_w, ((0, 0), (WINDOW, 0), (0, 0), (0, 0), (0, 0)))
    n_sel = -(-S // SEL_LEN)

    def block(bi):
        start = bi * Q_BLOCK
        qpos = start + jnp.arange(Q_BLOCK)

        def sl(a):
            return lax.dynamic_slice_in_dim(a, start, Q_BLOCK, axis=1)

        kvw = lax.dynamic_slice_in_dim(kv_w_pad, start, WINDOW + Q_BLOCK, axis=1)
        wpos = start - WINDOW + jnp.arange(WINDOW + Q_BLOCK)
        o_n = nsa_core(sl(q_n), qpos, kvc, lambda tok: gather_rows_heads(kv_s, tok), kvw, wpos, sl(g_n), n_sel, S)
        o_d = dsa_core(sl(q_d), qpos, sl(qi), sl(wi), ki, lambda tok: gather_rows(kv_d, tok))
        return o_n, o_d

    o_n, o_d = lax.map(block, jnp.arange(S // Q_BLOCK))
    o_n = jnp.swapaxes(o_n, 0, 1).reshape(B, S, H_NSA * HEAD_DIM)
    o_d = jnp.swapaxes(o_d, 0, 1).reshape(B, S, H_DSA * HEAD_DIM)
    win = kv_w[:, S - min(WINDOW, S):]
    return o_n, o_d, g_m, kv_c, kv_s, kv_d, ki, win


def sample_mixer(h, page_table, c_cmp, c_sel, c_dkv, c_didx, s_win, w_in_l, cmp_pos_l, cmp_w1_l, cmp_w2_l):
    B, S, _ = h.shape
    past_len = page_table.shape[1] * PAGE_SIZE
    L = past_len + S
    pos = past_len + jnp.arange(S)
    q_n, kv_c, kv_s, kv_w, g_n, q_d, kv_d, qi, ki, wi, g_m = project(h, w_in_l, pos)
    kvc = compress(logical_rows(c_cmp, page_table, kv_c), cmp_pos_l, cmp_w1_l, cmp_w2_l)
    wb = s_win.shape[1]
    kvw = jnp.concatenate([s_win, kv_w.astype(s_win.dtype)], axis=1)
    wpos = past_len - wb + jnp.arange(wb + S)
    n_sel = -(-L // SEL_LEN)
    o_n = nsa_core(q_n, pos, kvc, lambda tok: paged_rows_heads(c_sel, page_table, kv_s, tok), kvw, wpos, g_n, n_sel, L)
    ki_all = logical_rows(c_didx, page_table, ki)
    o_d = dsa_core(q_d, pos, qi, wi, ki_all, lambda tok: paged_rows(c_dkv, page_table, kv_d, tok))
    win = kvw[:, S:]
    return o_n, o_d, g_m, kv_c, kv_s, kv_d, ki, win


def merge(o_n, o_d, g_m, w_bn, w_bd, w_out):
    mix = g_m[:, :, 0] * (o_n @ w_bn) + g_m[:, :, 1] * (o_d @ w_bd)
    return mix @ w_out


def layer_tail(x, mix_out, p, norm_mlp, w_up, w_down, norm_ple, w_ple_gate, w_ple):
    x = x + mix_out
    u = rmsnorm(x, norm_mlp) @ w_up
    x = x + jnp.square(jax.nn.relu(u)) @ w_down
    gate = jax.nn.sigmoid((rmsnorm(x, norm_ple) @ w_ple_gate).astype(jnp.float32)).astype(x.dtype)
    return x + gate * (p @ w_ple)


def setup_inputs(seed: int = 0) -> dict:
    key = jax.random.key(seed)
    ks = jax.random.split(key, 26)
    f32 = jnp.float32
    n_pages = PAST_LEN // PAGE_SIZE
    n_used = DEC_BATCH * n_pages
    n_pool = n_used + -(-n_used // 4)
    win_buf = min(WINDOW, PAST_LEN)

    def nrm(k, shape, scale=1.0):
        return jax.random.normal(k, shape, f32) * scale

    def gain(k, shape):
        return 1.0 + 0.02 * jax.random.normal(k, shape, f32)

    page_table = jax.random.permutation(ks[7], n_pool)[:n_used].reshape(DEC_BATCH, n_pages).astype(jnp.int32)
    return {
        'x_prompt': nrm(ks[0], (BATCH, SEQ, D_MODEL)),
        'x_sample': nrm(ks[1], (DEC_BATCH, DEC_SEQ, D_MODEL)),
        'cache_nsa_cmp': nrm(ks[2], (DEPTH, n_pool, PAGE_SIZE, KV_NSA, 2, HEAD_DIM)),
        'cache_nsa_sel': nrm(ks[3], (DEPTH, n_pool, PAGE_SIZE, KV_NSA, 2, HEAD_DIM)),
        'cache_dsa_kv': nrm(ks[4], (DEPTH, n_pool, PAGE_SIZE, KV_DSA, 2, HEAD_DIM)),
        'cache_dsa_idx': nrm(ks[5], (DEPTH, n_pool, PAGE_SIZE, IDX_DIM)),
        'state_nsa_win': nrm(ks[6], (DEPTH, DEC_BATCH, win_buf, KV_NSA, 2, HEAD_DIM)),
        'page_table': page_table,
        'p_prompt': nrm(ks[8], (DEPTH, BATCH, SEQ, PLE_DIM)),
        'p_sample': nrm(ks[9], (DEPTH, DEC_BATCH, DEC_SEQ, PLE_DIM)),
        'norm_mix': gain(ks[10], (DEPTH, D_MODEL)),
        'w_in': nrm(ks[11], (DEPTH, D_MODEL, PROJ_W), D_MODEL ** -0.5),
        'cmp_pos': nrm(ks[12], (DEPTH, 2, CMP_LEN, HEAD_DIM), 0.1),
        'cmp_w1': nrm(ks[13], (DEPTH, 2, CMP_LEN, HEAD_DIM, CMP_HID), (CMP_LEN * HEAD_DIM) ** -0.5),
        'cmp_w2': nrm(ks[14], (DEPTH, 2, CMP_HID, HEAD_DIM), CMP_HID ** -0.5),
        'w_branch_nsa': nrm(ks[15], (DEPTH, H_NSA * HEAD_DIM, D_MODEL), (H_NSA * HEAD_DIM) ** -0.5),
        'w_branch_dsa': nrm(ks[16], (DEPTH, H_DSA * HEAD_DIM, D_MODEL), (H_DSA * HEAD_DIM) ** -0.5),
        'w_out': nrm(ks[17], (DEPTH, D_MODEL, D_MODEL), D_MODEL ** -0.5),
        'norm_mlp': gain(ks[18], (DEPTH, D_MODEL)),
        'w_up': nrm(ks[19], (DEPTH, D_MODEL, D_FF), D_MODEL ** -0.5),
        'w_down': nrm(ks[20], (DEPTH, D_FF, D_MODEL), D_FF ** -0.5),
        'norm_ple': gain(ks[21], (DEPTH, D_MODEL)),
        'w_ple_gate': nrm(ks[22], (DEPTH, D_MODEL, D_MODEL), D_MODEL ** -0.5),
        'w_ple': nrm(ks[23], (DEPTH, PLE_DIM, D_MODEL), PLE_DIM ** -0.5),
        'norm_final': gain(ks[24], (D_MODEL,)),
    }


def reference(x_prompt, x_sample, cache_nsa_cmp, cache_nsa_sel, cache_dsa_kv, cache_dsa_idx, state_nsa_win,
              page_table, p_prompt, p_sample, norm_mix, w_in, cmp_pos, cmp_w1, cmp_w2, w_branch_nsa, w_branch_dsa,
              w_out, norm_mlp, w_up, w_down, norm_ple, w_ple_gate, w_ple, norm_final):
    x_p, x_s = x_prompt, x_sample
    cmp_p, cmp_s, sel_p, sel_s, dkv_p, dkv_s, didx_p, didx_s, win_p, win_s = ([] for _ in range(10))
    for i in range(DEPTH):
        o_n, o_d, g_m, kc, kss, kd, ki, wn = prompt_mixer(rmsnorm(x_p, norm_mix[i]), w_in[i], cmp_pos[i], cmp_w1[i], cmp_w2[i])
        x_p = layer_tail(x_p, merge(o_n, o_d, g_m, w_branch_nsa[i], w_branch_dsa[i], w_out[i]), p_prompt[i],
                         norm_mlp[i], w_up[i], w_down[i], norm_ple[i], w_ple_gate[i], w_ple[i])
        cmp_p.append(kc); sel_p.append(kss); dkv_p.append(kd); didx_p.append(ki); win_p.append(wn)
        o_n, o_d, g_m, kc, kss, kd, ki, wn = sample_mixer(rmsnorm(x_s, norm_mix[i]), page_table, cache_nsa_cmp[i],
                                                          cache_nsa_sel[i], cache_dsa_kv[i], cache_dsa_idx[i],
                                                          state_nsa_win[i], w_in[i], cmp_pos[i], cmp_w1[i], cmp_w2[i])
        x_s = layer_tail(x_s, merge(o_n, o_d, g_m, w_branch_nsa[i], w_branch_dsa[i], w_out[i]), p_sample[i],
                         norm_mlp[i], w_up[i], w_down[i], norm_ple[i], w_ple_gate[i], w_ple[i])
        cmp_s.append(kc); sel_s.append(kss); dkv_s.append(kd); didx_s.append(ki); win_s.append(wn)
    y_prompt = rmsnorm(x_p, norm_final)
    y_sample = rmsnorm(x_s, norm_final)
    return (y_prompt, y_sample,
            jnp.stack(cmp_p), jnp.stack(cmp_s), jnp.stack(sel_p), jnp.stack(sel_s),
            jnp.stack(dkv_p), jnp.stack(dkv_s), jnp.stack(didx_p), jnp.stack(didx_s),
            jnp.stack(win_p), jnp.stack(win_s))
```

```python
import functools

import numpy as np
import jax
import jax.numpy as jnp
from jax import lax
from jax.experimental import pallas as pl
from jax.experimental.pallas import tpu as pltpu

F32 = jnp.float32
BF16 = jnp.bfloat16
I32 = jnp.int32

HEAD_DIM = 128
H_NSA = 16
H_DSA = 16
KV_NSA = 2
KV_DSA = 4
GRP_NSA = H_NSA // KV_NSA
GRP_DSA = H_DSA // KV_DSA
CMP_LEN = 32
CMP_STRIDE = 16
CMP_HID = 256
SEL_LEN = 64
SEL_TOP = 16
WINDOW = 512
IDX_HEADS = 16
IDX_DIM = 64
DSA_TOPK_MAX = 256
PAGE_SIZE = 128
ROPE_THETA = 10000.0
NORM_EPS = 1e-6
NEG_INF = -1e30
BELOW_NEG_INF = -3e38
FORCE_BONUS = 1e4

LANES = 128
Q_TILE = 128
K_TILE = 256
MISC_W = 1280
VMEM_LIMIT = 52 * 1024 * 1024
INT_MIN = -2 ** 31

ATTN_SCALE = HEAD_DIM ** -0.5
IDX_SCALE = (IDX_DIM ** -0.5) * (IDX_HEADS ** -0.5)


def _cparams(sem):
    return pltpu.CompilerParams(dimension_semantics=sem, vmem_limit_bytes=VMEM_LIMIT)


def _dot(a, b):
    return jnp.dot(a, b, preferred_element_type=F32)


def _dot_t(a, b):
    return lax.dot_general(a, b, (((1,), (1,)), ((), ())), preferred_element_type=F32)


def _sigmoid(x):
    return 1.0 / (1.0 + jnp.exp(-x))


def _rmsnorm_body(x_ref, g_ref, o_ref):
    x = x_ref[...]
    ms = jnp.mean(x * x, axis=-1, keepdims=True)
    o_ref[...] = (x * lax.rsqrt(ms + NORM_EPS) * g_ref[...]).astype(o_ref.dtype)


def _rmsnorm(x2d, g, out_dtype):
    m, d = x2d.shape
    tm = min(m, 256)
    return pl.pallas_call(
        _rmsnorm_body,
        grid=(m // tm,),
        in_specs=[pl.BlockSpec((tm, d), lambda i: (i, 0)),
                  pl.BlockSpec((1, d), lambda i: (0, 0))],
        out_specs=pl.BlockSpec((tm, d), lambda i: (i, 0)),
        out_shape=jax.ShapeDtypeStruct((m, d), out_dtype),
        compiler_params=_cparams(("parallel",)),
        name="rmsnorm",
    )(x2d, g.reshape(1, d))


def _mm_body(*refs, n_pairs, n_extra, n_out, nk, epi):
    a_refs = refs[:n_pairs]
    w_refs = refs[n_pairs:2 * n_pairs]
    e_refs = refs[2 * n_pairs:2 * n_pairs + n_extra]
    o_refs = refs[2 * n_pairs + n_extra:2 * n_pairs + n_extra + n_out]
    scratch = refs[2 * n_pairs + n_extra + n_out:]

    def dots():
        return [_dot(a[...].astype(BF16), w[...].astype(BF16)) for a, w in zip(a_refs, w_refs)]

    def finish(accs):
        val = epi(accs, e_refs)
        for o in o_refs:
            o[...] = val.astype(o.dtype)

    if nk == 1:
        finish(dots())
    else:
        acc_ref = scratch[0]
        k = pl.program_id(2)

        @pl.when(k == 0)
        def _():
            acc_ref[...] = jnp.zeros_like(acc_ref)

        acc_ref[...] += dots()[0]

        @pl.when(k == nk - 1)
        def _():
            finish([acc_ref[...]])


def _mm(a_list, w_list, extras, epi, out_dtypes, *, tm, tn, tk=None, name):
    m = a_list[0].shape[0]
    n = w_list[0].shape[1]
    tm = min(tm, m)
    tn = min(tn, n)
    assert m % tm == 0 and n % tn == 0
    nk = 1
    if tk is not None and tk < a_list[0].shape[1]:
        assert len(a_list) == 1 and a_list[0].shape[1] % tk == 0
        nk = a_list[0].shape[1] // tk
    in_specs = []
    for a in a_list:
        kk = a.shape[1]
        if nk == 1:
            in_specs.append(pl.BlockSpec((tm, kk), lambda i, j, k: (i, 0)))
        else:
            in_specs.append(pl.BlockSpec((tm, tk), lambda i, j, k: (i, k)))
    for w in w_list:
        kk = w.shape[0]
        if nk == 1:
            in_specs.append(pl.BlockSpec((kk, tn), lambda i, j, k: (0, j)))
        else:
            in_specs.append(pl.BlockSpec((tk, tn), lambda i, j, k: (k, j)))
    e_arrays = []
    for arr, kind in extras:
        e_arrays.append(arr)
        if kind == "row":
            nrow = arr.shape[0] // tm
            in_specs.append(pl.BlockSpec((tm, arr.shape[1]), lambda i, j, k, nrow=nrow: (i % nrow, 0)))
        elif kind == "col":
            in_specs.append(pl.BlockSpec((1, tn), lambda i, j, k: (0, j)))
        else:
            off = kind[1]
            in_specs.append(pl.BlockSpec((tm, tn), lambda i, j, k, off=off: (i, j + off)))
    out_specs = [pl.BlockSpec((tm, tn), lambda i, j, k: (i, j)) for _ in out_dtypes]
    out_shape = [jax.ShapeDtypeStruct((m, n), dt) for dt in out_dtypes]
    scratch = [pltpu.VMEM((tm, tn), F32)] if nk > 1 else []
    body = functools.partial(_mm_body, n_pairs=len(a_list), n_extra=len(extras),
                             n_out=len(out_dtypes), nk=nk, epi=epi)
    outs = pl.pallas_call(
        body,
        grid=(m // tm, n // tn, nk),
        in_specs=in_specs,
        out_specs=out_specs,
        out_shape=out_shape,
        scratch_shapes=scratch,
        compiler_params=_cparams(("parallel", "parallel", "arbitrary")),
        name=name,
    )(*a_list, *w_list, *e_arrays)
    return outs


def _rope128(a, cos, sin):
    return a * cos + pltpu.roll(a, HEAD_DIM // 2, 1) * sin


def _rope64(a, cos, sin, first_half):
    quarter = IDX_DIM // 2
    rot = jnp.where(first_half, pltpu.roll(a, LANES - quarter, 1), pltpu.roll(a, quarter, 1))
    return a * cos + rot * sin


def _epi_rope128(accs, e):
    acc = accs[0]
    cos, sin, flag = e[0][...], e[1][...], e[2][...]
    outs = []
    for h in range(acc.shape[1] // LANES):
        a = acc[:, h * LANES:(h + 1) * LANES]
        outs.append(jnp.where(flag[:, h * LANES:(h + 1) * LANES] > 0.5, _rope128(a, cos, sin), a))
    return jnp.concatenate(outs, axis=1)


def _epi_misc(accs, e):
    acc = accs[0]
    cos, sin = e[0][...], e[1][...]
    lane = lax.broadcasted_iota(I32, (1, LANES), 1)
    first_half = (lane % IDX_DIM) < (IDX_DIM // 2)
    n_q = IDX_HEADS * IDX_DIM // LANES
    outs = [_rope64(acc[:, c * LANES:(c + 1) * LANES], cos, sin, first_half) for c in range(n_q)]
    t = acc[:, n_q * LANES:(n_q + 1) * LANES]
    outs.append(jnp.where(lane < IDX_DIM, _rope64(t, cos, sin, first_half),
                          jnp.where(lane < IDX_DIM + IDX_HEADS, t, _sigmoid(t))))
    outs.append(_rope64(acc[:, (n_q + 1) * LANES:(n_q + 2) * LANES], cos, sin, first_half))
    return jnp.concatenate(outs, axis=1)


def _epi_sigmoid(accs, e):
    return _sigmoid(accs[0])


def _epi_merge(accs, e):
    return accs[0] * e[0][...] + accs[1] * e[1][...]


def _epi_residual(accs, e):
    return e[0][...] + accs[0]


def _epi_relu2(accs, e):
    return jnp.square(jnp.maximum(accs[0], 0.0))


def _epi_ple(accs, e):
    return e[0][...] + _sigmoid(accs[0]) * accs[1]


def _compress_body(pt_ref, rows_hbm, pos_ref, w1_ref, w2_ref, o_ref, buf, sem, *, npages, page_rows, nc):
    b = pl.program_id(0)
    gc_rows = KV_NSA * 2
    chunk_rows = CMP_STRIDE * gc_rows

    def page_copy(p):
        src = rows_hbm.at[pl.ds(pl.multiple_of(pt_ref[b * npages + p] * page_rows, page_rows), page_rows)]
        dst = buf.at[pl.ds(pl.multiple_of(p * page_rows, page_rows), page_rows)]
        return pltpu.make_async_copy(src, dst, sem)

    def start(p, c):
        page_copy(p).start()
        return c

    lax.fori_loop(0, npages, start, 0)
    buf[pl.ds(npages * page_rows, chunk_rows), :] = jnp.zeros((chunk_rows, LANES), F32)

    def wait(p, c):
        page_copy(p).wait()
        return c

    lax.fori_loop(0, npages, wait, 0)

    for gc in range(gc_rows):
        c = gc % 2
        xa = jnp.concatenate(
            [(buf[pl.ds(gc_rows * l + gc, nc, stride=chunk_rows), :] + pos_ref[c, l:l + 1, :]).astype(BF16)
             for l in range(CMP_STRIDE)], axis=1)
        xb = jnp.concatenate(
            [(buf[pl.ds(gc_rows * l + gc + chunk_rows, nc, stride=chunk_rows), :]
              + pos_ref[c, CMP_STRIDE + l:CMP_STRIDE + l + 1, :]).astype(BF16)
             for l in range(CMP_STRIDE)], axis=1)
        hid = jnp.maximum(_dot(xa, w1_ref[c, 0]) + _dot(xb, w1_ref[c, 1]), 0.0)
        o_ref[0, :, gc * LANES:(gc + 1) * LANES] = _dot(hid.astype(BF16), w2_ref[c])


def _compress(rows, page_table, page_rows, cmp_pos, w1, w2):
    nseq, npages = page_table.shape
    gc_rows = KV_NSA * 2
    chunk_rows = CMP_STRIDE * gc_rows
    nc = npages * page_rows // chunk_rows
    body = functools.partial(_compress_body, npages=npages, page_rows=page_rows, nc=nc)
    grid_spec = pltpu.PrefetchScalarGridSpec(
        num_scalar_prefetch=1,
        grid=(nseq,),
        in_specs=[pl.BlockSpec(memory_space=pl.ANY),
                  pl.BlockSpec((2, CMP_LEN, HEAD_DIM), lambda b, pt: (0, 0, 0)),
                  pl.BlockSpec((2, 2, CMP_STRIDE * HEAD_DIM, CMP_HID), lambda b, pt: (0, 0, 0, 0)),
                  pl.BlockSpec((2, CMP_HID, HEAD_DIM), lambda b, pt: (0, 0, 0))],
        out_specs=pl.BlockSpec((1, nc, gc_rows * HEAD_DIM), lambda b, pt: (b, 0, 0)),
        scratch_shapes=[pltpu.VMEM((npages * page_rows + chunk_rows, LANES), F32),
                        pltpu.SemaphoreType.DMA(())],
    )
    return pl.pallas_call(
        body,
        grid_spec=grid_spec,
        out_shape=jax.ShapeDtypeStruct((nseq, nc, gc_rows * HEAD_DIM), F32),
        compiler_params=_cparams(("arbitrary",)),
        name="compress",
    )(page_table.reshape(-1).astype(I32), rows, cmp_pos, w1, w2)


def _flash(q_rows, kv_fn, mask_fn, lo, hi, m_scr, l_scr, acc_scr, n_rep):
    m_scr[...] = jnp.full(m_scr.shape, NEG_INF, F32)
    l_scr[...] = jnp.zeros(l_scr.shape, F32)
    acc_scr[...] = jnp.zeros(acc_scr.shape, F32)

    def step(kt, carry):
        k, v = kv_fn(kt)
        s = _dot_t(q_rows, k) * ATTN_SCALE
        keep = jnp.concatenate([mask_fn(kt)] * n_rep, axis=0) > 0.5
        sm = jnp.where(keep, s, NEG_INF)
        m_old = m_scr[...]
        m_new = jnp.maximum(m_old, jnp.max(sm, axis=-1, keepdims=True))
        alpha = jnp.exp(m_old - m_new)
        p = jnp.where(keep, jnp.exp(sm - m_new), 0.0)
        l_scr[...] = alpha * l_scr[...] + jnp.sum(p, axis=-1, keepdims=True)
        acc_scr[...] = alpha * acc_scr[...] + _dot(p.astype(BF16), v)
        m_scr[...] = m_new
        return carry

    lax.fori_loop(lo, hi, step, 0)
    l = l_scr[...]
    return acc_scr[...] * jnp.where(l > 0.0, 1.0 / l, 0.0)


def _masked_softmax(s, keep):
    sm = jnp.where(keep, s, NEG_INF)
    m = jnp.max(sm, axis=-1, keepdims=True)
    p = jnp.where(keep, jnp.exp(sm - m), 0.0)
    den = jnp.sum(p, axis=-1, keepdims=True)
    return p * jnp.where(den > 0.0, 1.0 / den, 0.0)


def _importance(psum, ov_ref):
    hi = psum.astype(BF16)
    lo = (psum - hi.astype(F32)).astype(BF16)
    ov = ov_ref[...]
    return _dot(hi, ov) + _dot(lo, ov)


def _selection_scores(imp, qpos, n_sel):
    sb = lax.broadcasted_iota(I32, (1, imp.shape[1]), 1)
    cur = lax.shift_right_logical(qpos, int(np.log2(SEL_LEN)))
    forced = jnp.where(sb == 0, FORCE_BONUS,
                       jnp.where(sb == cur, FORCE_BONUS, jnp.where(sb == cur - 1, FORCE_BONUS, 0.0)))
    score = jnp.where(sb <= cur, imp + forced, NEG_INF)
    return jnp.where(sb < n_sel, score, BELOW_NEG_INF)


def _nsa_prompt_body(q_ref, kvc_ref, kvs_ref, kvw_ref, gate_ref, ov_ref, ex_ref, o_ref,
                     mask_scr, m_scr, l_scr, acc_scr, *, tq, seq, n_sel, n_top):
    qi = pl.program_id(2)
    q0 = qi * tq
    rep = GRP_NSA
    q_rows = jnp.concatenate([q_ref[0, :, r * LANES:(r + 1) * LANES] for r in range(rep)], axis=0)
    row = lax.broadcasted_iota(I32, (rep * tq, 1), 0)
    qpos_rows = q0 + (row & (tq - 1))
    qpos = q0 + lax.broadcasted_iota(I32, (tq, 1), 0)

    kc = kvc_ref[0, :, 0:HEAD_DIM].astype(BF16)
    vc = kvc_ref[0, :, HEAD_DIM:2 * HEAD_DIM].astype(BF16)
    s = _dot_t(q_rows, kc) * ATTN_SCALE
    blk = lax.broadcasted_iota(I32, (1, kc.shape[0]), 1)
    pc = _masked_softmax(s, (blk * CMP_STRIDE + (CMP_LEN - 1)) <= qpos_rows)
    o_c = _dot(pc.astype(BF16), vc)

    psum = pc[0:tq]
    for r in range(1, rep):
        psum = psum + pc[r * tq:(r + 1) * tq]
    score = _selection_scores(_importance(psum, ov_ref), qpos, n_sel)
    sb = lax.broadcasted_iota(I32, (1, score.shape[1]), 1)
    rank = jnp.zeros(score.shape, F32)
    for j in range(n_sel):
        col = score[:, j:j + 1]
        rank = rank + jnp.where(col > score, 1.0, jnp.where(col == score, jnp.where(sb > j, 1.0, 0.0), 0.0))
    sel = jnp.where(rank < n_top, 1.0, 0.0).astype(BF16)
    sel_tok = _dot(sel, ex_ref[...])
    tpos = lax.broadcasted_iota(I32, (1, seq), 1)
    keep = jnp.where(sel_tok > 0.5, jnp.where(tpos <= qpos, 1.0, 0.0), 0.0)
    for kt in range(seq // K_TILE):
        mask_scr[kt] = keep[:, kt * K_TILE:(kt + 1) * K_TILE]

    def sel_kv(kt):
        rows = pl.ds(pl.multiple_of(kt * K_TILE, K_TILE), K_TILE)
        return kvs_ref[0, rows, 0:HEAD_DIM], kvs_ref[0, rows, HEAD_DIM:2 * HEAD_DIM]

    o_s = _flash(q_rows, sel_kv, lambda kt: mask_scr[kt], 0, (q0 + tq - 1) // K_TILE + 1,
                 m_scr, l_scr, acc_scr, rep)

    def win_kv(kt):
        rows = pl.ds(pl.multiple_of(kt * tq, tq), tq)
        return kvw_ref[0, rows, 0:HEAD_DIM], kvw_ref[0, rows, HEAD_DIM:2 * HEAD_DIM]

    def win_mask(kt):
        d = qpos - (kt * tq + lax.broadcasted_iota(I32, (1, tq), 1))
        return jnp.where(d >= 0, jnp.where(d < WINDOW, 1.0, 0.0), 0.0)

    o_w = _flash(q_rows, win_kv, win_mask, jnp.maximum(qi - WINDOW // tq, 0), qi + 1,
                 m_scr, l_scr, acc_scr, rep)

    g = gate_ref[0, 0]
    for r in range(rep):
        rows = slice(r * tq, (r + 1) * tq)
        o = (g[:, 3 * r:3 * r + 1] * o_c[rows] + g[:, 3 * r + 1:3 * r + 2] * o_s[rows]
             + g[:, 3 * r + 2:3 * r + 3] * o_w[rows])
        o_ref[0, :, r * LANES:(r + 1) * LANES] = o.astype(o_ref.dtype)


def _overlap_matrix(nc_pad, nc_real, n_sel, lanes):
    n = np.arange(nc_pad)[:, None]
    j = np.arange(lanes)[None, :]
    ov = ((n * CMP_STRIDE < (j + 1) * SEL_LEN) & (n * CMP_STRIDE + CMP_LEN > j * SEL_LEN)
          & (n < nc_real) & (j < n_sel))
    return jnp.asarray(ov.astype(np.float32), dtype=BF16)


def _nsa_prompt(qn, kvc, kvs_bf, kvw_bf, gates, seq):
    bsz = qn.shape[0]
    tq = Q_TILE
    assert seq % K_TILE == 0 and WINDOW % tq == 0
    n_sel = -(-seq // SEL_LEN)
    assert n_sel <= LANES
    n_top = min(SEL_TOP, n_sel)
    nc_pad = kvc.shape[1]
    nc_real = (seq - CMP_LEN) // CMP_STRIDE + 1
    ov = _overlap_matrix(nc_pad, nc_real, n_sel, LANES)
    ex = jnp.asarray((np.arange(seq)[None, :] // SEL_LEN == np.arange(LANES)[:, None]).astype(np.float32), dtype=BF16)
    gw = GRP_NSA * HEAD_DIM
    body = functools.partial(_nsa_prompt_body, tq=tq, seq=seq, n_sel=n_sel, n_top=n_top)
    return pl.pallas_call(
        body,
        grid=(bsz, KV_NSA, seq // tq),
        in_specs=[pl.BlockSpec((1, tq, gw), lambda b, g, i: (b, i, g)),
                  pl.BlockSpec((1, nc_pad, 2 * HEAD_DIM), lambda b, g, i: (b, 0, g)),
                  pl.BlockSpec((1, seq, 2 * HEAD_DIM), lambda b, g, i: (b, 0, g)),
                  pl.BlockSpec((1, seq, 2 * HEAD_DIM), lambda b, g, i: (b, 0, g)),
                  pl.BlockSpec((1, 1, tq, LANES), lambda b, g, i: (b, g, i, 0)),
                  pl.BlockSpec((nc_pad, LANES), lambda b, g, i: (0, 0)),
                  pl.BlockSpec((LANES, seq), lambda b, g, i: (0, 0))],
        out_specs=pl.BlockSpec((1, tq, gw), lambda b, g, i: (b, i, g)),
        out_shape=jax.ShapeDtypeStruct((bsz, seq, H_NSA * HEAD_DIM), BF16),
        scratch_shapes=[pltpu.VMEM((seq // K_TILE, tq, K_TILE), F32),
                        pltpu.VMEM((GRP_NSA * tq, 1), F32),
                        pltpu.VMEM((GRP_NSA * tq, 1), F32),
                        pltpu.VMEM((GRP_NSA * tq, HEAD_DIM), F32)],
        compiler_params=_cparams(("parallel", "parallel", "arbitrary")),
        name="nsa_prompt",
    )(qn, kvc, kvs_bf, kvw_bf, gates, ov, ex)


def _ordered_key(x):
    bits = pltpu.bitcast(x + 0.0, I32)
    return jnp.where(bits >= 0, bits, bits ^ jnp.int32(0x7FFFFFFF))


def _radix_kth(count_ge, shape, k):
    zero = jnp.zeros(shape, I32)
    p = jnp.where(count_ge(zero) >= k, zero, jnp.full(shape, INT_MIN, I32))

    def bit_step(i, p):
        cand = p + lax.shift_left(jnp.int32(1), jnp.int32(30) - i)
        return jnp.where(count_ge(cand) >= k, cand, p)

    return lax.fori_loop(0, 31, bit_step, p)


def _tie_cutoff(count_eq_below, need, shape, nbits):
    def bit_step(i, c):
        cand = c + lax.shift_left(jnp.int32(1), jnp.int32(nbits - 1) - i)
        return jnp.where(count_eq_below(cand) < need, cand, c)

    return lax.fori_loop(0, nbits, bit_step, jnp.zeros(shape, I32))


def _dsa_prompt_body(qd_ref, kvd_ref, mq_ref, ki2_ref, o_ref,
                     key_scr, mask_scr, cut_scr, m_scr, l_scr, acc_scr, *, tq, seq, topk):
    qi = pl.program_id(1)
    q0 = qi * tq
    nkt = (q0 + tq - 1) // K_TILE + 1
    qpos = q0 + lax.broadcasted_iota(I32, (tq, 1), 0)
    lane = lax.broadcasted_iota(I32, (1, LANES), 1)
    n_q = IDX_HEADS * IDX_DIM // LANES
    mq = mq_ref[0]
    wrow = mq[:, n_q * LANES + IDX_DIM:n_q * LANES + IDX_DIM + IDX_HEADS] * IDX_SCALE
    parts = []
    for c in range(n_q):
        a = mq[:, c * LANES:(c + 1) * LANES]
        parts.append(jnp.where(lane < IDX_DIM, a, 0.0).astype(BF16))
        parts.append(jnp.where(lane >= IDX_DIM, a, 0.0).astype(BF16))
    q_idx = jnp.concatenate(parts, axis=0)

    def tile_pos(kt):
        return kt * K_TILE + lax.broadcasted_iota(I32, (1, K_TILE), 1)

    def score_step(kt, carry):
        k2 = ki2_ref[0, pl.ds(pl.multiple_of(kt * K_TILE, K_TILE), K_TILE), :].astype(BF16)
        sc = _dot_t(q_idx, k2)
        acc = jnp.zeros((tq, K_TILE), F32)
        for h in range(IDX_HEADS):
            acc = acc + jnp.maximum(sc[h * tq:(h + 1) * tq], 0.0) * wrow[:, h:h + 1]
        key_scr[kt] = _ordered_key(jnp.where(tile_pos(kt) <= qpos, acc, NEG_INF))
        return carry

    lax.fori_loop(0, nkt, score_step, 0)

    def count(flag_fn):
        def step(kt, acc):
            return acc + flag_fn(kt)
        acc = lax.fori_loop(0, nkt, step, jnp.zeros((tq, K_TILE), F32))
        return jnp.sum(acc, axis=-1, keepdims=True)

    thr = _radix_kth(lambda cand: count(lambda kt: jnp.where(key_scr[kt] >= cand, 1.0, 0.0)), (tq, 1), topk)
    n_ge = count(lambda kt: jnp.where(key_scr[kt] >= thr, 1.0, 0.0))
    n_gt = count(lambda kt: jnp.where(key_scr[kt] > thr, 1.0, 0.0))
    cut_scr[...] = jnp.full((tq, 1), seq, I32)

    @pl.when(jnp.max(n_ge) > topk)
    def _():
        def count_eq_below(c):
            return count(lambda kt: jnp.where(key_scr[kt] == thr, jnp.where(tile_pos(kt) < c, 1.0, 0.0), 0.0))
        cut_scr[...] = _tie_cutoff(count_eq_below, topk - n_gt, (tq, 1), int(seq).bit_length())

    cut = cut_scr[...]

    def mask_step(kt, carry):
        key = key_scr[kt]
        t = tile_pos(kt)
        chosen = jnp.where(key > thr, 1.0, jnp.where(key == thr, jnp.where(t <= cut, 1.0, 0.0), 0.0))
        mask_scr[kt] = jnp.where(t <= qpos, chosen, 0.0)
        return carry

    lax.fori_loop(0, nkt, mask_step, 0)

    rep = GRP_DSA
    for gd in range(KV_DSA):
        q_rows = jnp.concatenate(
            [qd_ref[0, :, (gd * rep + r) * LANES:(gd * rep + r + 1) * LANES] for r in range(rep)], axis=0)

        def kv(kt, gd=gd):
            rows = pl.ds(pl.multiple_of(kt * K_TILE, K_TILE), K_TILE)
            return (kvd_ref[0, rows, 2 * gd * HEAD_DIM:(2 * gd + 1) * HEAD_DIM],
                    kvd_ref[0, rows, (2 * gd + 1) * HEAD_DIM:(2 * gd + 2) * HEAD_DIM])

        o = _flash(q_rows, kv, lambda kt: mask_scr[kt], 0, nkt, m_scr, l_scr, acc_scr, rep)
        for r in range(rep):
            o_ref[0, :, (gd * rep + r) * LANES:(gd * rep + r + 1) * LANES] = o[r * tq:(r + 1) * tq].astype(o_ref.dtype)


def _dsa_prompt(qd, kvd_bf, misc, seq):
    bsz = qd.shape[0]
    tq = Q_TILE
    assert seq % K_TILE == 0
    topk = min(DSA_TOPK_MAX, seq // 4)
    assert topk <= K_TILE
    body = functools.partial(_dsa_prompt_body, tq=tq, seq=seq, topk=topk)
    nkt = seq // K_TILE
    return pl.pallas_call(
        body,
        grid=(bsz, seq // tq),
        in_specs=[pl.BlockSpec((1, tq, H_DSA * HEAD_DIM), lambda b, i: (b, i, 0)),
                  pl.BlockSpec((1, seq, KV_DSA * 2 * HEAD_DIM), lambda b, i: (b, 0, 0)),
                  pl.BlockSpec((1, tq, MISC_W), lambda b, i: (b, i, 0)),
                  pl.BlockSpec((1, seq, LANES), lambda b, i: (b, 0, MISC_W // LANES - 1))],
        out_specs=pl.BlockSpec((1, tq, H_DSA * HEAD_DIM), lambda b, i: (b, i, 0)),
        out_shape=jax.ShapeDtypeStruct((bsz, seq, H_DSA * HEAD_DIM), BF16),
        scratch_shapes=[pltpu.VMEM((nkt, tq, K_TILE), I32),
                        pltpu.VMEM((nkt, tq, K_TILE), F32),
                        pltpu.VMEM((tq, 1), I32),
                        pltpu.VMEM((GRP_DSA * tq, 1), F32),
                        pltpu.VMEM((GRP_DSA * tq, 1), F32),
                        pltpu.VMEM((GRP_DSA * tq, HEAD_DIM), F32)],
        compiler_params=_cparams(("parallel", "arbitrary")),
        name="dsa_prompt",
    )(qd, kvd_bf, misc, misc)


def _online_update(m_ref, l_ref, acc_ref, rows, s, keep, pv_fn):
    sm = s if keep is None else jnp.where(keep, s, NEG_INF)
    m_old = m_ref[rows, :]
    m_new = jnp.maximum(m_old, jnp.max(sm, axis=-1, keepdims=True))
    alpha = jnp.exp(m_old - m_new)
    p = jnp.exp(sm - m_new)
    if keep is not None:
        p = jnp.where(keep, p, 0.0)
    l_ref[rows, :] = alpha * l_ref[rows, :] + jnp.sum(p, axis=-1, keepdims=True)
    acc_ref[rows, :] = alpha * acc_ref[rows, :] + pv_fn(p)
    m_ref[rows, :] = m_new


def _bf16_round(x):
    return x.astype(BF16).astype(F32)


def _nsa_sample_cmp_body(q_ref, kvc_ref, ov_ref, oc_ref, sc_ref, *, qpos, n_sel):
    sc_ref[...] = jnp.zeros(sc_ref.shape, F32)
    rep = GRP_NSA
    for g in range(KV_NSA):
        q = q_ref[0, g * rep:(g + 1) * rep, :]
        kc = kvc_ref[0, :, 2 * g * HEAD_DIM:(2 * g + 1) * HEAD_DIM].astype(BF16)
        vc = kvc_ref[0, :, (2 * g + 1) * HEAD_DIM:(2 * g + 2) * HEAD_DIM].astype(BF16)
        s = _dot_t(q, kc) * ATTN_SCALE
        blk = lax.broadcasted_iota(I32, (1, kc.shape[0]), 1)
        pc = _masked_softmax(s, (blk * CMP_STRIDE + (CMP_LEN - 1)) <= qpos)
        oc_ref[0, g * rep:(g + 1) * rep, :] = _dot(pc.astype(BF16), vc)
        psum = jnp.sum(pc, axis=0, keepdims=True)
        imp = _importance(psum, ov_ref)
        sc_ref[0, g:g + 1, :] = _selection_scores(imp, jnp.full((1, 1), qpos, I32), n_sel)


def _topk_index_body(s_ref, o_ref, *, n_top):
    s = s_ref[...]
    lane = lax.broadcasted_iota(I32, (1, s.shape[1]), 1).astype(F32)
    olane = lax.broadcasted_iota(I32, (1, o_ref.shape[1]), 1)
    out = jnp.zeros(o_ref.shape, F32)
    for i in range(n_top):
        m = jnp.max(s, axis=-1, keepdims=True)
        idx = jnp.min(jnp.where(s == m, lane, 1e9), axis=-1, keepdims=True)
        out = jnp.where(olane == i, idx, out)
        s = jnp.where(lane == idx, -jnp.inf, s)
    o_ref[...] = out.astype(I32)


def _nsa_sample_attn_body(sel_ref, pt_ref, q_ref, new_ref, gate_ref, oc_ref, win_ref, xa_ref, xb_ref, o_ref,
                          m_scr, l_scr, acc_scr, *, n_top, n_past_blk):
    b = pl.program_id(0)
    i = pl.program_id(1)
    rep = GRP_NSA
    gc_rows = KV_NSA * 2

    @pl.when(i == 0)
    def _():
        m_scr[...] = jnp.full(m_scr.shape, NEG_INF, F32)
        l_scr[...] = jnp.zeros(l_scr.shape, F32)
        acc_scr[...] = jnp.zeros(acc_scr.shape, F32)

    for g, x_ref in enumerate((xa_ref, xb_ref)):
        rows = slice(g * rep, (g + 1) * rep)
        j = sel_ref[(b * KV_NSA + g) * n_top + i]
        q = q_ref[0, rows, :]

        @pl.when(j < n_past_blk)
        def _(g=g, rows=rows, q=q, x_ref=x_ref):
            k = x_ref[pl.ds(2 * g, SEL_LEN, stride=gc_rows), :].astype(BF16)
            v = x_ref[pl.ds(2 * g + 1, SEL_LEN, stride=gc_rows), :].astype(BF16)
            _online_update(m_scr, l_scr, acc_scr, rows, _dot_t(q, k) * ATTN_SCALE, None,
                           lambda p: _dot(p.astype(BF16), v))

        @pl.when(j >= n_past_blk)
        def _(g=g, rows=rows, q=q):
            k = _bf16_round(new_ref[0, :, 2 * g * HEAD_DIM:(2 * g + 1) * HEAD_DIM])
            v = _bf16_round(new_ref[0, :, (2 * g + 1) * HEAD_DIM:(2 * g + 2) * HEAD_DIM])
            s = jnp.sum(q.astype(F32) * k, axis=-1, keepdims=True) * ATTN_SCALE
            _online_update(m_scr, l_scr, acc_scr, rows, s, None, lambda p: _bf16_round(p) * v)

    @pl.when(i == n_top - 1)
    def _():
        l = l_scr[...]
        o_s = acc_scr[...] * jnp.where(l > 0.0, 1.0 / l, 0.0)
        gate = gate_ref[0]
        for g in range(KV_NSA):
            rows = slice(g * rep, (g + 1) * rep)
            q = q_ref[0, rows, :]
            kw = win_ref[0, :, 2 * g * HEAD_DIM:(2 * g + 1) * HEAD_DIM].astype(BF16)
            vw = win_ref[0, :, (2 * g + 1) * HEAD_DIM:(2 * g + 2) * HEAD_DIM].astype(BF16)
            s = _dot_t(q, kw) * ATTN_SCALE
            pw = jnp.exp(s - jnp.max(s, axis=-1, keepdims=True))
            pw = pw * (1.0 / jnp.sum(pw, axis=-1, keepdims=True))
            o_w = _dot(pw.astype(BF16), vw)
            gt = gate[rows, :]
            o = gt[:, 0:1] * oc_ref[0, rows, :] + gt[:, 1:2] * o_s[rows] + gt[:, 2:3] * o_w
            o_ref[0, rows, :] = o.astype(o_ref.dtype)


def _dsa_sample_score_body(pt_ref, q_ref, w_ref, kp_ref, kn_ref, o_ref, *, npages):
    p = pl.program_id(1)
    q = q_ref[0].astype(BF16)
    w = w_ref[0] * IDX_SCALE

    def emit(k):
        sc = _dot_t(q, k.astype(BF16))
        o_ref[0, 0] = jnp.sum(jnp.maximum(sc, 0.0) * w, axis=0, keepdims=True)

    @pl.when(p < npages)
    def _():
        emit(kp_ref[0])

    @pl.when(p == npages)
    def _():
        emit(kn_ref[0])


def _dsa_sample_mask_body(s_ref, o_ref, cut_scr, *, length, topk):
    x = s_ref[...]
    shape = (x.shape[0], 1)
    t = lax.broadcasted_iota(I32, (1, x.shape[1]), 1)
    key = _ordered_key(jnp.where(t < length, x, BELOW_NEG_INF))

    def count(flags):
        return jnp.sum(flags, axis=-1, keepdims=True)

    thr = _radix_kth(lambda cand: count(jnp.where(key >= cand, 1.0, 0.0)), shape, topk)
    n_ge = count(jnp.where(key >= thr, 1.0, 0.0))
    n_gt = count(jnp.where(key > thr, 1.0, 0.0))
    cut_scr[...] = jnp.full(shape, x.shape[1], I32)

    @pl.when(jnp.max(n_ge) > topk)
    def _():
        cut_scr[...] = _tie_cutoff(
            lambda c: count(jnp.where(key == thr, jnp.where(t < c, 1.0, 0.0), 0.0)), topk - n_gt, shape,
            int(x.shape[1]).bit_length())

    cut = cut_scr[...]
    o_ref[...] = jnp.where(key > thr, 1.0, jnp.where(key == thr, jnp.where(t <= cut, 1.0, 0.0), 0.0))


def _dsa_sample_attn_body(pt_ref, q_ref, mask_ref, mnew_ref, new_ref, x_ref, o_ref, m_scr, l_scr, acc_scr, *, npages):
    p = pl.program_id(1)
    rep = GRP_DSA
    gc_rows = KV_DSA * 2

    @pl.when(p == 0)
    def _():
        m_scr[...] = jnp.full(m_scr.shape, NEG_INF, F32)
        l_scr[...] = jnp.zeros(l_scr.shape, F32)
        acc_scr[...] = jnp.zeros(acc_scr.shape, F32)

    keep = mask_ref[0, 0] > 0.5
    for g in range(KV_DSA):
        rows = slice(g * rep, (g + 1) * rep)
        k = x_ref[pl.ds(2 * g, PAGE_SIZE, stride=gc_rows), :].astype(BF16)
        v = x_ref[pl.ds(2 * g + 1, PAGE_SIZE, stride=gc_rows), :].astype(BF16)
        s = _dot_t(q_ref[0, rows, :], k) * ATTN_SCALE
        _online_update(m_scr, l_scr, acc_scr, rows, s, jnp.broadcast_to(keep, s.shape),
                       lambda pr, v=v: _dot(pr.astype(BF16), v))

    @pl.when(p == npages - 1)
    def _():
        keep_new = mnew_ref[0, 0][:, 0:1] > 0.5
        for g in range(KV_DSA):
            rows = slice(g * rep, (g + 1) * rep)
            k = _bf16_round(new_ref[0, :, 2 * g * HEAD_DIM:(2 * g + 1) * HEAD_DIM])
            v = _bf16_round(new_ref[0, :, (2 * g + 1) * HEAD_DIM:(2 * g + 2) * HEAD_DIM])
            s = jnp.sum(q_ref[0, rows, :].astype(F32) * k, axis=-1, keepdims=True) * ATTN_SCALE
            _online_update(m_scr, l_scr, acc_scr, rows, s, jnp.broadcast_to(keep_new, s.shape),
                           lambda pr, v=v: _bf16_round(pr) * v)
        l = l_scr[...]
        o_ref[0] = (acc_scr[...] * jnp.where(l > 0.0, 1.0 / l, 0.0)).astype(o_ref.dtype)


def _rope_tables(pos, dim):
    half = dim // 2
    inv = ROPE_THETA ** (-jnp.arange(half, dtype=F32) / half)
    ang = pos.astype(F32)[:, None] * inv[None, :]
    cos, sin = jnp.cos(ang), jnp.sin(ang)
    reps = LANES // dim
    return (jnp.tile(jnp.concatenate([cos, cos], axis=-1), (1, reps)),
            jnp.tile(jnp.concatenate([-sin, sin], axis=-1), (1, reps)))


def _project(h, w, pos, tm):
    d = h.shape[1]
    splits = (H_NSA * HEAD_DIM, KV_NSA * 2 * HEAD_DIM, KV_NSA * 2 * HEAD_DIM, KV_NSA * 2 * HEAD_DIM, 3 * H_NSA,
              H_DSA * HEAD_DIM, KV_DSA * 2 * HEAD_DIM, IDX_HEADS * IDX_DIM, IDX_DIM, IDX_HEADS, 2 * d)
    offs = np.concatenate([[0], np.cumsum(splits)])
    seg = lambda i: w[:, int(offs[i]):int(offs[i + 1])].astype(BF16)
    w_qn, w_kvc, w_kvs, w_kvw, w_gn, w_qd, w_kvd, w_qi, w_ki, w_wi, w_gm = (seg(i) for i in range(11))
    w_misc = jnp.concatenate([w_qi, w_ki, w_wi, w_gn, w_ki, w_ki], axis=1)
    assert w_misc.shape[1] == MISC_W
    cos128, sin128 = _rope_tables(pos, HEAD_DIM)
    cos64, sin64 = _rope_tables(pos, IDX_DIM)
    rope_all = lambda n: jnp.ones((1, n), F32)
    rope_k = lambda n: jnp.tile(jnp.concatenate([jnp.ones((1, HEAD_DIM), F32), jnp.zeros((1, HEAD_DIM), F32)], 1),
                                (1, n // (2 * HEAD_DIM)))

    def rope_mm(wseg, flag, dtypes, name):
        return _mm([h], [wseg], [(cos128, "row"), (sin128, "row"), (flag, "col")], _epi_rope128, dtypes,
                   tm=tm, tn=512, name=name)

    out = {}
    out["qn"], = rope_mm(w_qn, rope_all(w_qn.shape[1]), [BF16], "proj_qn")
    out["kvc"], = rope_mm(w_kvc, rope_k(w_kvc.shape[1]), [F32], "proj_kvc")
    out["kvs"], out["kvs_bf"] = rope_mm(w_kvs, rope_k(w_kvs.shape[1]), [F32, BF16], "proj_kvs")
    out["kvw"], out["kvw_bf"] = rope_mm(w_kvw, rope_k(w_kvw.shape[1]), [F32, BF16], "proj_kvw")
    out["qd"], = rope_mm(w_qd, rope_all(w_qd.shape[1]), [BF16], "proj_qd")
    out["kvd"], out["kvd_bf"] = rope_mm(w_kvd, rope_k(w_kvd.shape[1]), [F32, BF16], "proj_kvd")
    out["misc"], = _mm([h], [w_misc], [(cos64, "row"), (sin64, "row")], _epi_misc, [F32],
                       tm=min(tm, 512), tn=MISC_W, name="proj_misc")
    out["gm"], = _mm([h], [w_gm], [], _epi_sigmoid, [F32], tm=tm, tn=512, name="proj_gm")
    return out


def _merge_and_tail(x, o_n, o_d, gm, p, wts, tm):
    d = x.shape[1]
    tn = 512
    mix, = _mm([o_n, o_d], [wts["w_bn"], wts["w_bd"]], [(gm, ("tile", 0)), (gm, ("tile", d // tn))],
               _epi_merge, [BF16], tm=tm, tn=tn, name="merge")
    x1, = _mm([mix], [wts["w_out"]], [(x, ("tile", 0))], _epi_residual, [F32], tm=tm, tn=tn, name="out_proj")
    h2 = _rmsnorm(x1, wts["norm_mlp"], BF16)
    u, = _mm([h2], [wts["w_up"]], [], _epi_relu2, [BF16], tm=tm, tn=tn, name="mlp_up")
    x2, = _mm([u], [wts["w_down"]], [(x1, ("tile", 0))], _epi_residual, [F32], tm=tm, tn=1024, tk=2048,
              name="mlp_down")
    h3 = _rmsnorm(x2, wts["norm_ple"], BF16)
    x3, = _mm([h3, p], [wts["w_ple_gate"], wts["w_ple"]], [(x2, ("tile", 0))], _epi_ple, [F32],
              tm=tm, tn=tn, name="ple")
    return _rmsnorm(x3, wts["norm_final"], F32)


def kernel(x_prompt, x_sample, cache_nsa_cmp, cache_nsa_sel, cache_dsa_kv, cache_dsa_idx, state_nsa_win, page_table, p_prompt, p_sample, norm_mix, w_in, cmp_pos, cmp_w1, cmp_w2, w_branch_nsa, w_branch_dsa, w_out, norm_mlp, w_up, w_down, norm_ple, w_ple_gate, w_ple, norm_final):
    depth = w_in.shape[0]
    assert depth == 1
    bsz, seq, d = x_prompt.shape
    nb, dec_seq, _ = x_sample.shape
    assert dec_seq == 1
    npages = page_table.shape[1]
    past = npages * PAGE_SIZE
    length = past + dec_seq
    n_pool = cache_nsa_cmp.shape[1]
    gc_n = KV_NSA * 2
    gc_d = KV_DSA * 2

    wts = {
        "w_bn": w_branch_nsa[0].astype(BF16), "w_bd": w_branch_dsa[0].astype(BF16),
        "w_out": w_out[0].astype(BF16), "w_up": w_up[0].astype(BF16), "w_down": w_down[0].astype(BF16),
        "w_ple_gate": w_ple_gate[0].astype(BF16), "w_ple": w_ple[0].astype(BF16),
        "norm_mlp": norm_mlp[0], "norm_ple": norm_ple[0], "norm_final": norm_final,
    }
    w1 = cmp_w1[0].reshape(2, 2, CMP_STRIDE * HEAD_DIM, CMP_HID).astype(BF16)
    w2 = cmp_w2[0].astype(BF16)
    pos_emb = cmp_pos[0]

    mp = bsz * seq
    xp = x_prompt.reshape(mp, d)
    hp = _rmsnorm(xp, norm_mix[0], BF16)
    pr = _project(hp, w_in[0], jnp.arange(seq), tm=1024)

    chunk_rows = CMP_STRIDE * gc_n
    pages_p = seq * gc_n // (PAGE_SIZE * gc_n)
    kvc_p = _compress(pr["kvc"].reshape(mp * gc_n, HEAD_DIM),
                      jnp.arange(bsz * pages_p, dtype=I32).reshape(bsz, pages_p), PAGE_SIZE * gc_n, pos_emb, w1, w2)
    n_q = IDX_HEADS * IDX_DIM
    gn_p = pr["misc"][:, n_q + IDX_DIM + IDX_HEADS:n_q + IDX_DIM + IDX_HEADS + 3 * H_NSA]
    gates_p = jnp.pad(gn_p.reshape(bsz, seq, KV_NSA, 3 * GRP_NSA).transpose(0, 2, 1, 3),
                      ((0, 0), (0, 0), (0, 0), (0, LANES - 3 * GRP_NSA)))
    o_n_p = _nsa_prompt(pr["qn"].reshape(bsz, seq, -1), kvc_p, pr["kvs_bf"].reshape(bsz, seq, -1),
                        pr["kvw_bf"].reshape(bsz, seq, -1), gates_p, seq)
    o_d_p = _dsa_prompt(pr["qd"].reshape(bsz, seq, -1), pr["kvd_bf"].reshape(bsz, seq, -1),
                        pr["misc"].reshape(bsz, seq, MISC_W), seq)
    y_p = _merge_and_tail(xp, o_n_p.reshape(mp, -1), o_d_p.reshape(mp, -1), pr["gm"],
                          p_prompt[0].reshape(mp, -1), wts, tm=1024)

    xs = x_sample.reshape(nb, d)
    hs = _rmsnorm(xs, norm_mix[0], BF16)
    sm = _project(hs, w_in[0], jnp.full((nb,), past, I32), tm=nb)
    misc_s = sm["misc"]
    pt_flat = page_table.reshape(-1).astype(I32)

    kvc_s = _compress(cache_nsa_cmp[0].reshape(n_pool * PAGE_SIZE * gc_n, HEAD_DIM), page_table,
                      PAGE_SIZE * gc_n, pos_emb, w1, w2)
    nc_s = kvc_s.shape[1]
    n_sel_s = -(-length // SEL_LEN)
    n_top_s = min(SEL_TOP, n_sel_s)
    sel_lanes = -(-n_sel_s // LANES) * LANES
    ov_s = _overlap_matrix(nc_s, (length - CMP_LEN) // CMP_STRIDE + 1, n_sel_s, sel_lanes)
    qn_s = sm["qn"].reshape(nb, H_NSA, HEAD_DIM)
    oc_s, score_s = pl.pallas_call(
        functools.partial(_nsa_sample_cmp_body, qpos=length - 1, n_sel=n_sel_s),
        grid=(nb,),
        in_specs=[pl.BlockSpec((1, H_NSA, HEAD_DIM), lambda b: (b, 0, 0)),
                  pl.BlockSpec((1, nc_s, gc_n * HEAD_DIM), lambda b: (b, 0, 0)),
                  pl.BlockSpec((nc_s, sel_lanes), lambda b: (0, 0))],
        out_specs=[pl.BlockSpec((1, H_NSA, HEAD_DIM), lambda b: (b, 0, 0)),
                   pl.BlockSpec((1, 8, sel_lanes), lambda b: (b, 0, 0))],
        out_shape=[jax.ShapeDtypeStruct((nb, H_NSA, HEAD_DIM), F32),
                   jax.ShapeDtypeStruct((nb, 8, sel_lanes), F32)],
        compiler_params=_cparams(("parallel",)),
        name="nsa_sample_cmp",
    )(qn_s, kvc_s, ov_s)
    sel_idx = pl.pallas_call(
        functools.partial(_topk_index_body, n_top=n_top_s),
        out_shape=jax.ShapeDtypeStruct((nb * KV_NSA, LANES), I32),
        compiler_params=pltpu.CompilerParams(vmem_limit_bytes=VMEM_LIMIT),
        name="nsa_sample_topk",
    )(score_s[:, :KV_NSA].reshape(nb * KV_NSA, sel_lanes))[:, :n_top_s].reshape(-1)

    n_past_blk = past // SEL_LEN
    blk_rows = SEL_LEN * gc_n
    blk_per_page = PAGE_SIZE // SEL_LEN
    win_s = jnp.concatenate([state_nsa_win[0], sm["kvw"].reshape(nb, 1, KV_NSA, 2, HEAD_DIM)], axis=1)[:, dec_seq:]
    wb = win_s.shape[1]
    gn_s = misc_s[:, n_q + IDX_DIM + IDX_HEADS:n_q + IDX_DIM + IDX_HEADS + 3 * H_NSA]
    gates_s = jnp.pad(gn_s.reshape(nb, H_NSA, 3), ((0, 0), (0, 0), (0, LANES - 3)))
    sel_rows = cache_nsa_sel[0].reshape(n_pool * PAGE_SIZE * gc_n, HEAD_DIM)

    def sel_block(g):
        def index_map(b, i, sel, pt):
            j = jnp.minimum(sel[(b * KV_NSA + g) * n_top_s + i], n_past_blk - 1)
            return (pt[b * npages + j // blk_per_page] * blk_per_page + j % blk_per_page, 0)
        return pl.BlockSpec((blk_rows, HEAD_DIM), index_map)

    o_n_s = pl.pallas_call(
        functools.partial(_nsa_sample_attn_body, n_top=n_top_s, n_past_blk=n_past_blk),
        grid_spec=pltpu.PrefetchScalarGridSpec(
            num_scalar_prefetch=2,
            grid=(nb, n_top_s),
            in_specs=[pl.BlockSpec((1, H_NSA, HEAD_DIM), lambda b, i, sel, pt: (b, 0, 0)),
                      pl.BlockSpec((1, 1, gc_n * HEAD_DIM), lambda b, i, sel, pt: (b, 0, 0)),
                      pl.BlockSpec((1, H_NSA, LANES), lambda b, i, sel, pt: (b, 0, 0)),
                      pl.BlockSpec((1, H_NSA, HEAD_DIM), lambda b, i, sel, pt: (b, 0, 0)),
                      pl.BlockSpec((1, wb, gc_n * HEAD_DIM), lambda b, i, sel, pt: (b, 0, 0)),
                      sel_block(0), sel_block(1)],
            out_specs=pl.BlockSpec((1, H_NSA, HEAD_DIM), lambda b, i, sel, pt: (b, 0, 0)),
            scratch_shapes=[pltpu.VMEM((H_NSA, 1), F32), pltpu.VMEM((H_NSA, 1), F32),
                            pltpu.VMEM((H_NSA, HEAD_DIM), F32)]),
        out_shape=jax.ShapeDtypeStruct((nb, H_NSA, HEAD_DIM), BF16),
        compiler_params=_cparams(("parallel", "arbitrary")),
        name="nsa_sample_attn",
    )(sel_idx, pt_flat, qn_s, sm["kvs"].reshape(nb, 1, -1), gates_s, oc_s,
      win_s.reshape(nb, wb, gc_n * HEAD_DIM), sel_rows, sel_rows)

    qi_s = misc_s[:, :n_q].reshape(nb, IDX_HEADS, IDX_DIM)
    wi_s = misc_s[:, n_q + IDX_DIM:n_q + IDX_DIM + IDX_HEADS].reshape(nb, IDX_HEADS, 1)
    ki_new = jnp.pad(misc_s[:, n_q:n_q + IDX_DIM].reshape(nb, 1, IDX_DIM), ((0, 0), (0, PAGE_SIZE - 1), (0, 0)))
    idx_s = pl.pallas_call(
        functools.partial(_dsa_sample_score_body, npages=npages),
        grid_spec=pltpu.PrefetchScalarGridSpec(
            num_scalar_prefetch=1,
            grid=(nb, npages + 1),
            in_specs=[pl.BlockSpec((1, IDX_HEADS, IDX_DIM), lambda b, p, pt: (b, 0, 0)),
                      pl.BlockSpec((1, IDX_HEADS, 1), lambda b, p, pt: (b, 0, 0)),
                      pl.BlockSpec((1, PAGE_SIZE, IDX_DIM),
                                   lambda b, p, pt: (pt[b * npages + jnp.minimum(p, npages - 1)], 0, 0)),
                      pl.BlockSpec((1, PAGE_SIZE, IDX_DIM), lambda b, p, pt: (b, 0, 0))],
            out_specs=pl.BlockSpec((1, 1, 1, PAGE_SIZE), lambda b, p, pt: (b, p, 0, 0))),
        out_shape=jax.ShapeDtypeStruct((nb, npages + 1, 1, PAGE_SIZE), F32),
        compiler_params=_cparams(("parallel", "arbitrary")),
        name="dsa_sample_score",
    )(pt_flat, qi_s, wi_s, cache_dsa_idx[0], ki_new)
    topk_s = min(DSA_TOPK_MAX, length // 4)
    width = (npages + 1) * PAGE_SIZE
    mask_s = pl.pallas_call(
        functools.partial(_dsa_sample_mask_body, length=length, topk=topk_s),
        out_shape=jax.ShapeDtypeStruct((nb, width), F32),
        scratch_shapes=[pltpu.VMEM((nb, 1), I32)],
        compiler_params=pltpu.CompilerParams(vmem_limit_bytes=VMEM_LIMIT),
        name="dsa_sample_mask",
    )(idx_s.reshape(nb, width)).reshape(nb, npages + 1, 1, PAGE_SIZE)
    page_rows_d = PAGE_SIZE * gc_d
    o_d_s = pl.pallas_call(
        functools.partial(_dsa_sample_attn_body, npages=npages),
        grid_spec=pltpu.PrefetchScalarGridSpec(
            num_scalar_prefetch=1,
            grid=(nb, npages),
            in_specs=[pl.BlockSpec((1, H_DSA, HEAD_DIM), lambda b, p, pt: (b, 0, 0)),
                      pl.BlockSpec((1, 1, 1, PAGE_SIZE), lambda b, p, pt: (b, p, 0, 0)),
                      pl.BlockSpec((1, 1, 1, PAGE_SIZE), lambda b, p, pt: (b, npages, 0, 0)),
                      pl.BlockSpec((1, 1, gc_d * HEAD_DIM), lambda b, p, pt: (b, 0, 0)),
                      pl.BlockSpec((page_rows_d, HEAD_DIM), lambda b, p, pt: (pt[b * npages + p], 0))],
            out_specs=pl.BlockSpec((1, H_DSA, HEAD_DIM), lambda b, p, pt: (b, 0, 0)),
            scratch_shapes=[pltpu.VMEM((H_DSA, 1), F32), pltpu.VMEM((H_DSA, 1), F32),
                            pltpu.VMEM((H_DSA, HEAD_DIM), F32)]),
        out_shape=jax.ShapeDtypeStruct((nb, H_DSA, HEAD_DIM), BF16),
        compiler_params=_cparams(("parallel", "arbitrary")),
        name="dsa_sample_attn",
    )(pt_flat, sm["qd"].reshape(nb, H_DSA, HEAD_DIM), mask_s, mask_s, sm["kvd"].reshape(nb, 1, -1),
      cache_dsa_kv[0].reshape(n_pool * page_rows_d, HEAD_DIM))

    y_s = _merge_and_tail(xs, o_n_s.reshape(nb, -1), o_d_s.reshape(nb, -1), sm["gm"],
                          p_sample[0].reshape(nb, -1), wts, tm=nb)

    kv6 = lambda a, n, s, g: a.reshape(1, n, s, g, 2, HEAD_DIM)
    n_q_end = n_q + IDX_DIM
    return (y_p.reshape(bsz, seq, d), y_s.reshape(nb, dec_seq, d),
            kv6(pr["kvc"], bsz, seq, KV_NSA), kv6(sm["kvc"], nb, dec_seq, KV_NSA),
            kv6(pr["kvs"], bsz, seq, KV_NSA), kv6(sm["kvs"], nb, dec_seq, KV_NSA),
            kv6(pr["kvd"], bsz, seq, KV_DSA), kv6(sm["kvd"], nb, dec_seq, KV_DSA),
            pr["misc"][:, n_q:n_q_end].reshape(1, bsz, seq, IDX_DIM),
            misc_s[:, n_q:n_q_end].reshape(1, nb, dec_seq, IDX_DIM),
            kv6(pr["kvw"], bsz, seq, KV_NSA)[:, :, seq - min(WINDOW, seq):],
            win_s[None])
```

```python
import functools

import numpy as np
import jax
import jax.numpy as jnp
from jax import lax
from jax.experimental import pallas as pl
from jax.experimental.pallas import tpu as pltpu

F32 = jnp.float32
BF16 = jnp.bfloat16
I32 = jnp.int32

HEAD_DIM = 128
H_NSA = 16
H_DSA = 16
KV_NSA = 2
KV_DSA = 4
GRP_NSA = H_NSA // KV_NSA
GRP_DSA = H_DSA // KV_DSA
CMP_LEN = 32
CMP_STRIDE = 16
CMP_HID = 256
SEL_LEN = 64
SEL_TOP = 16
WINDOW = 512
IDX_HEADS = 16
IDX_DIM = 64
DSA_TOPK_MAX = 256
PAGE_SIZE = 128
ROPE_THETA = 10000.0
NORM_EPS = 1e-6
NEG_INF = -1e30
BELOW_NEG_INF = -3e38
FORCE_BONUS = 1e4

LANES = 128
Q_TILE = 128
K_TILE = 256
DSA_GROUPS_PER_PASS = 4
MISC_W = 1280
VMEM_LIMIT = 52 * 1024 * 1024
INT_MIN = -2 ** 31

ATTN_SCALE = HEAD_DIM ** -0.5
IDX_SCALE = (IDX_DIM ** -0.5) * (IDX_HEADS ** -0.5)
LOG2E = float(np.log2(np.e))
EXP2_SCALE = ATTN_SCALE * LOG2E


def _cparams(sem):
    return pltpu.CompilerParams(dimension_semantics=sem, vmem_limit_bytes=VMEM_LIMIT)


def _dot(a, b):
    return jnp.dot(a, b, preferred_element_type=F32)


def _dot_t(a, b):
    return lax.dot_general(a, b, (((1,), (1,)), ((), ())), preferred_element_type=F32)


def _sigmoid(x):
    return 1.0 / (1.0 + jnp.exp(-x))


def _div_pow2(x, n):
    assert n & (n - 1) == 0
    return lax.shift_right_logical(x, int(np.log2(n)))


def _mod_pow2(x, n):
    assert n & (n - 1) == 0
    return x & (n - 1)


def _rmsnorm_body(x_ref, g_ref, o_ref):
    x = x_ref[...]
    ms = jnp.mean(x * x, axis=-1, keepdims=True)
    o_ref[...] = (x * lax.rsqrt(ms + NORM_EPS) * g_ref[...]).astype(o_ref.dtype)


def _rmsnorm(x2d, g, out_dtype):
    m, d = x2d.shape
    tm = min(m, 256)
    return pl.pallas_call(
        _rmsnorm_body,
        grid=(m // tm,),
        in_specs=[pl.BlockSpec((tm, d), lambda i: (i, 0)),
                  pl.BlockSpec((1, d), lambda i: (0, 0))],
        out_specs=pl.BlockSpec((tm, d), lambda i: (i, 0)),
        out_shape=jax.ShapeDtypeStruct((m, d), out_dtype),
        compiler_params=_cparams(("parallel",)),
        name="rmsnorm",
    )(x2d, g.reshape(1, d))


def _mm_body(*refs, n_pairs, n_extra, n_out, nk, epi):
    a_refs = refs[:n_pairs]
    w_refs = refs[n_pairs:2 * n_pairs]
    e_refs = refs[2 * n_pairs:2 * n_pairs + n_extra]
    o_refs = refs[2 * n_pairs + n_extra:2 * n_pairs + n_extra + n_out]
    scratch = refs[2 * n_pairs + n_extra + n_out:]

    def dots():
        return [_dot(a[...].astype(BF16), w[...].astype(BF16)) for a, w in zip(a_refs, w_refs)]

    def finish(accs):
        val = epi(accs, e_refs)
        for o in o_refs:
            o[...] = val.astype(o.dtype)

    if nk == 1:
        finish(dots())
    else:
        acc_ref = scratch[0]
        k = pl.program_id(2)

        @pl.when(k == 0)
        def _():
            acc_ref[...] = jnp.zeros_like(acc_ref)

        acc_ref[...] += dots()[0]

        @pl.when(k == nk - 1)
        def _():
            finish([acc_ref[...]])


def _mm(a_list, w_list, extras, epi, out_dtypes, *, tm, tn, tk=None, name):
    m = a_list[0].shape[0]
    n = w_list[0].shape[1]
    tm = min(tm, m)
    tn = min(tn, n)
    assert m % tm == 0 and n % tn == 0
    nk = 1
    if tk is not None and tk < a_list[0].shape[1]:
        assert len(a_list) == 1 and a_list[0].shape[1] % tk == 0
        nk = a_list[0].shape[1] // tk
    in_specs = []
    for a in a_list:
        kk = a.shape[1]
        if nk == 1:
            in_specs.append(pl.BlockSpec((tm, kk), lambda i, j, k: (i, 0)))
        else:
            in_specs.append(pl.BlockSpec((tm, tk), lambda i, j, k: (i, k)))
    for w in w_list:
        kk = w.shape[0]
        if nk == 1:
            in_specs.append(pl.BlockSpec((kk, tn), lambda i, j, k: (0, j)))
        else:
            in_specs.append(pl.BlockSpec((tk, tn), lambda i, j, k: (k, j)))
    e_arrays = []
    for arr, kind in extras:
        e_arrays.append(arr)
        if kind == "row":
            nrow = arr.shape[0] // tm
            in_specs.append(pl.BlockSpec((tm, arr.shape[1]), lambda i, j, k, nrow=nrow: (i % nrow, 0)))
        elif kind == "col":
            in_specs.append(pl.BlockSpec((1, tn), lambda i, j, k: (0, j)))
        else:
            off = kind[1]
            in_specs.append(pl.BlockSpec((tm, tn), lambda i, j, k, off=off: (i, j + off)))
    out_specs = [pl.BlockSpec((tm, tn), lambda i, j, k: (i, j)) for _ in out_dtypes]
    out_shape = [jax.ShapeDtypeStruct((m, n), dt) for dt in out_dtypes]
    scratch = [pltpu.VMEM((tm, tn), F32)] if nk > 1 else []
    body = functools.partial(_mm_body, n_pairs=len(a_list), n_extra=len(extras),
                             n_out=len(out_dtypes), nk=nk, epi=epi)
    outs = pl.pallas_call(
        body,
        grid=(m // tm, n // tn, nk),
        in_specs=in_specs,
        out_specs=out_specs,
        out_shape=out_shape,
        scratch_shapes=scratch,
        compiler_params=_cparams(("parallel", "parallel", "arbitrary")),
        name=name,
    )(*a_list, *w_list, *e_arrays)
    return outs


def _rope128(a, cos, sin):
    return a * cos + pltpu.roll(a, HEAD_DIM // 2, 1) * sin


def _rope64(a, cos, sin, first_half):
    quarter = IDX_DIM // 2
    rot = jnp.where(first_half, pltpu.roll(a, LANES - quarter, 1), pltpu.roll(a, quarter, 1))
    return a * cos + rot * sin


def _epi_rope128(accs, e):
    acc = accs[0]
    cos, sin, flag = e[0][...], e[1][...], e[2][...]
    outs = []
    for h in range(acc.shape[1] // LANES):
        a = acc[:, h * LANES:(h + 1) * LANES]
        outs.append(jnp.where(flag[:, h * LANES:(h + 1) * LANES] > 0.5, _rope128(a, cos, sin), a))
    return jnp.concatenate(outs, axis=1)


def _epi_misc(accs, e):
    acc = accs[0]
    cos, sin = e[0][...], e[1][...]
    lane = lax.broadcasted_iota(I32, (1, LANES), 1)
    first_half = (lane % IDX_DIM) < (IDX_DIM // 2)
    n_q = IDX_HEADS * IDX_DIM // LANES
    outs = [_rope64(acc[:, c * LANES:(c + 1) * LANES], cos, sin, first_half) for c in range(n_q)]
    t = acc[:, n_q * LANES:(n_q + 1) * LANES]
    outs.append(jnp.where(lane < IDX_DIM, _rope64(t, cos, sin, first_half),
                          jnp.where(lane < IDX_DIM + IDX_HEADS, t, _sigmoid(t))))
    outs.append(_rope64(acc[:, (n_q + 1) * LANES:(n_q + 2) * LANES], cos, sin, first_half))
    return jnp.concatenate(outs, axis=1)


def _epi_sigmoid(accs, e):
    return _sigmoid(accs[0])


def _epi_merge(accs, e):
    return accs[0] * e[0][...] + accs[1] * e[1][...]


def _epi_residual(accs, e):
    return e[0][...] + accs[0]


def _epi_relu2(accs, e):
    return jnp.square(jnp.maximum(accs[0], 0.0))


def _epi_ple(accs, e):
    return e[0][...] + _sigmoid(accs[0]) * accs[1]


def _chunk_pitch(chunk_rows):
    return chunk_rows + 8


def _compress_body(pt_ref, rows_hbm, pos_ref, w1_ref, w2_ref, o_ref, buf, sem, *, npages, page_rows, nc):
    b = pl.program_id(0)
    gc_rows = KV_NSA * 2
    chunk_rows = CMP_STRIDE * gc_rows
    chunks_per_page = page_rows // chunk_rows
    pitch = _chunk_pitch(chunk_rows)

    def chunk_copy(i):
        page = pt_ref[b * npages + _div_pow2(i, chunks_per_page)]
        src_row = page * page_rows + _mod_pow2(i, chunks_per_page) * chunk_rows
        src = rows_hbm.at[pl.ds(pl.multiple_of(src_row, chunk_rows), chunk_rows)]
        dst = buf.at[pl.ds(pl.multiple_of(i * pitch, 8), chunk_rows)]
        return pltpu.make_async_copy(src, dst, sem)

    def start(i, c):
        chunk_copy(i).start()
        return c

    lax.fori_loop(0, nc, start, 0)
    buf[pl.ds(nc * pitch, chunk_rows), :] = jnp.zeros((chunk_rows, LANES), F32)

    def wait(i, c):
        chunk_copy(i).wait()
        return c

    lax.fori_loop(0, nc, wait, 0)

    for gc in range(gc_rows):
        c = gc % 2
        xa = jnp.concatenate(
            [(buf[pl.ds(gc_rows * l + gc, nc, stride=pitch), :] + pos_ref[c, l:l + 1, :]).astype(BF16)
             for l in range(CMP_STRIDE)], axis=1)
        xb = jnp.concatenate(
            [(buf[pl.ds(gc_rows * l + gc + pitch, nc, stride=pitch), :]
              + pos_ref[c, CMP_STRIDE + l:CMP_STRIDE + l + 1, :]).astype(BF16)
             for l in range(CMP_STRIDE)], axis=1)
        hid = jnp.maximum(_dot(xa, w1_ref[c, 0]) + _dot(xb, w1_ref[c, 1]), 0.0)
        o_ref[0, :, gc * LANES:(gc + 1) * LANES] = _dot(hid.astype(BF16), w2_ref[c])


def _compress(rows, page_table, page_rows, cmp_pos, w1, w2):
    nseq, npages = page_table.shape
    gc_rows = KV_NSA * 2
    chunk_rows = CMP_STRIDE * gc_rows
    nc = npages * page_rows // chunk_rows
    body = functools.partial(_compress_body, npages=npages, page_rows=page_rows, nc=nc)
    grid_spec = pltpu.PrefetchScalarGridSpec(
        num_scalar_prefetch=1,
        grid=(nseq,),
        in_specs=[pl.BlockSpec(memory_space=pl.ANY),
                  pl.BlockSpec((2, CMP_LEN, HEAD_DIM), lambda b, pt: (0, 0, 0)),
                  pl.BlockSpec((2, 2, CMP_STRIDE * HEAD_DIM, CMP_HID), lambda b, pt: (0, 0, 0, 0)),
                  pl.BlockSpec((2, CMP_HID, HEAD_DIM), lambda b, pt: (0, 0, 0))],
        out_specs=pl.BlockSpec((1, nc, gc_rows * HEAD_DIM), lambda b, pt: (b, 0, 0)),
        scratch_shapes=[pltpu.VMEM(((nc + 1) * _chunk_pitch(chunk_rows), LANES), F32),
                        pltpu.SemaphoreType.DMA(())],
    )
    return pl.pallas_call(
        body,
        grid_spec=grid_spec,
        out_shape=jax.ShapeDtypeStruct((nseq, nc, gc_rows * HEAD_DIM), F32),
        compiler_params=_cparams(("arbitrary",)),
        name="compress",
    )(page_table.reshape(-1).astype(I32), rows, cmp_pos, w1, w2)


def _softmax_cols(s):
    m = jnp.max(s, axis=0, keepdims=True)
    p = jnp.exp2((s - jnp.where(m == -jnp.inf, 0.0, m)) * LOG2E)
    den = jnp.sum(p, axis=0, keepdims=True)
    return p * jnp.where(den > 0.0, 1.0 / den, 0.0)


def _flash_t(q_t, k_fn, vt_fn, neg_fn, lo, hi, m_scr, l_scr, acc_scr):
    w = q_t.shape[1]
    chunk = 2 * Q_TILE
    m_scr[...] = jnp.full(m_scr.shape, -jnp.inf, F32)
    l_scr[...] = jnp.zeros(l_scr.shape, F32)
    acc_scr[...] = jnp.zeros(acc_scr.shape, F32)

    def step(kt, carry):
        ks = k_fn(kt)
        vts = vt_fn(kt)
        neg = neg_fn(kt)
        neg2 = jnp.concatenate([neg, neg], axis=1)
        n_chunk = w // chunk
        per_group = n_chunk // len(ks)
        cols = [slice(c * chunk, (c + 1) * chunk) for c in range(n_chunk)]
        m_old = m_scr[...]
        l_old = l_scr[...]
        s = [_dot(ks[c // per_group], q_t[:, cols[c]]) + neg2 for c in range(n_chunk)]
        m_new = jnp.maximum(m_old, jnp.concatenate([jnp.max(sc, axis=0, keepdims=True) for sc in s], axis=1))
        m_ref = jnp.where(m_new == -jnp.inf, 0.0, m_new)
        alpha = jnp.exp2((m_old - m_ref) * EXP2_SCALE)
        p = [jnp.exp2((s[c] - m_ref[:, cols[c]]) * EXP2_SCALE) for c in range(n_chunk)]
        l_scr[...] = alpha * l_old + jnp.concatenate([jnp.sum(pc, axis=0, keepdims=True) for pc in p], axis=1)
        m_scr[...] = m_new
        pv = [_dot(vts[c // per_group], p[c].astype(BF16)) for c in range(n_chunk)]
        for c in range(n_chunk):
            acc_scr[:, cols[c]] = alpha[:, cols[c]] * acc_scr[:, cols[c]] + pv[c]
        return carry

    lax.fori_loop(lo, hi, step, 0)
    l = l_scr[...]
    return acc_scr[...] * jnp.where(l > 0.0, 1.0 / l, 0.0)


def _masked_softmax(s, keep):
    sm = jnp.where(keep, s, NEG_INF)
    m = jnp.max(sm, axis=-1, keepdims=True)
    p = jnp.where(keep, jnp.exp(sm - m), 0.0)
    den = jnp.sum(p, axis=-1, keepdims=True)
    return p * jnp.where(den > 0.0, 1.0 / den, 0.0)


def _importance(psum, ov_ref):
    hi = psum.astype(BF16)
    lo = (psum - hi.astype(F32)).astype(BF16)
    ov = ov_ref[...]
    return _dot(hi, ov) + _dot(lo, ov)


def _selection_scores(imp, qpos, n_sel):
    sb = lax.broadcasted_iota(I32, (1, imp.shape[1]), 1)
    cur = lax.shift_right_logical(qpos, int(np.log2(SEL_LEN)))
    forced = jnp.where(sb == 0, FORCE_BONUS,
                       jnp.where(sb == cur, FORCE_BONUS, jnp.where(sb == cur - 1, FORCE_BONUS, 0.0)))
    score = jnp.where(sb <= cur, imp + forced, NEG_INF)
    return jnp.where(sb < n_sel, score, BELOW_NEG_INF)


def _nsa_prompt_body(q_ref, kvc_ref, ks_ref, vst_ref, kw_ref, vwt_ref, gate_ref, ovt_ref, o_ref,
                     sel_scr, m_scr, l_scr, acc_scr, *, tq, n_sel, n_top):
    qi = pl.program_id(2)
    q0 = qi * tq
    rep = GRP_NSA
    w = rep * tq
    chunk = 2 * tq
    q_t = jnp.concatenate([q_ref[0, r] for r in range(rep)], axis=1)
    qpos = q0 + lax.broadcasted_iota(I32, (1, tq), 1)
    qpos2 = jnp.concatenate([qpos, qpos], axis=1)

    kc = kvc_ref[0, :, 0:HEAD_DIM].astype(BF16)
    vct = kvc_ref[0, :, HEAD_DIM:2 * HEAD_DIM].T.astype(BF16)
    ncp = kc.shape[0]
    blk_end = lax.broadcasted_iota(I32, (ncp, 1), 0) * CMP_STRIDE + (CMP_LEN - 1)
    neg_c = jnp.where(blk_end <= qpos2, 0.0, -jnp.inf)
    psum = jnp.zeros((ncp, tq), F32)
    oc_parts = []
    for c in range(w // chunk):
        pc = _softmax_cols(_dot(kc, q_t[:, c * chunk:(c + 1) * chunk]) * ATTN_SCALE + neg_c)
        oc_parts.append(_dot(vct, pc.astype(BF16)))
        psum = psum + pc[:, :tq] + pc[:, tq:]
    o_ct = jnp.concatenate(oc_parts, axis=1)

    hi = psum.astype(BF16)
    lo = (psum - hi.astype(F32)).astype(BF16)
    ovt = ovt_ref[...]
    imp = _dot(ovt, hi) + _dot(ovt, lo)
    sb = lax.broadcasted_iota(I32, (imp.shape[0], 1), 0)
    cur = lax.shift_right_logical(qpos, int(np.log2(SEL_LEN)))
    forced = jnp.where(sb == 0, FORCE_BONUS,
                       jnp.where(sb == cur, FORCE_BONUS, jnp.where(sb == cur - 1, FORCE_BONUS, 0.0)))
    score = jnp.where(sb < n_sel, jnp.where(sb <= cur, imp + forced, NEG_INF), BELOW_NEG_INF)
    rank = jnp.zeros(score.shape, F32)
    for j in range(n_sel):
        row = score[j:j + 1, :]
        rank = rank + jnp.where(row > score, 1.0, jnp.where(row == score, jnp.where(sb > j, 1.0, 0.0), 0.0))
    sel_scr[...] = jnp.where(rank < n_top, 0.0, -jnp.inf)

    blocks_per_tile = K_TILE // SEL_LEN

    def sel_neg(kt):
        t = kt * K_TILE + lax.broadcasted_iota(I32, (K_TILE, 1), 0)
        blk_neg = jnp.concatenate(
            [jnp.broadcast_to(sel_scr[pl.ds(kt * blocks_per_tile + c, 1), :], (SEL_LEN, tq))
             for c in range(blocks_per_tile)], axis=0)
        return jnp.where(t <= qpos, blk_neg, -jnp.inf)

    def tile(ref, kt, size):
        return ref[0, 0, pl.ds(pl.multiple_of(kt * size, size), size), :]

    def tile_t(ref, kt, size):
        return ref[0, 0, :, pl.ds(pl.multiple_of(kt * size, size), size)]

    o_st = _flash_t(q_t, lambda kt: [tile(ks_ref, kt, K_TILE)], lambda kt: [tile_t(vst_ref, kt, K_TILE)], sel_neg,
                    0, (q0 + tq - 1) // K_TILE + 1, m_scr, l_scr, acc_scr)

    def win_neg(kt):
        d = qpos - (kt * tq + lax.broadcasted_iota(I32, (tq, 1), 0))
        return jnp.where(d >= 0, jnp.where(d < WINDOW, 0.0, -jnp.inf), -jnp.inf)

    o_wt = _flash_t(q_t, lambda kt: [tile(kw_ref, kt, tq)], lambda kt: [tile_t(vwt_ref, kt, tq)], win_neg,
                    jnp.maximum(qi - WINDOW // tq, 0), qi + 1, m_scr, l_scr, acc_scr)

    g = gate_ref[0, 0]
    for r in range(rep):
        cols = slice(r * tq, (r + 1) * tq)
        o_t = (g[3 * r:3 * r + 1] * o_ct[:, cols] + g[3 * r + 1:3 * r + 2] * o_st[:, cols]
               + g[3 * r + 2:3 * r + 3] * o_wt[:, cols])
        o_ref[0, :, r * LANES:(r + 1) * LANES] = o_t.T.astype(o_ref.dtype)


def _overlap_matrix(nc_pad, nc_real, n_sel, lanes):
    n = np.arange(nc_pad)[:, None]
    j = np.arange(lanes)[None, :]
    ov = ((n * CMP_STRIDE < (j + 1) * SEL_LEN) & (n * CMP_STRIDE + CMP_LEN > j * SEL_LEN)
          & (n < nc_real) & (j < n_sel))
    return jnp.asarray(ov.astype(np.float32), dtype=BF16)


def _split_kv_t(kv_bf, bsz, seq, groups):
    kv = kv_bf.reshape(bsz, seq, groups, 2, HEAD_DIM)
    return kv[:, :, :, 0].transpose(0, 2, 1, 3), kv[:, :, :, 1].transpose(0, 2, 3, 1)


def _nsa_prompt(qn, kvc, kvs_bf, kvw_bf, gn, bsz, seq):
    tq = Q_TILE
    assert seq % K_TILE == 0 and WINDOW % tq == 0
    n_sel = -(-seq // SEL_LEN)
    n_top = min(SEL_TOP, n_sel)
    nsp = -(-n_sel // 8) * 8
    nc_pad = kvc.shape[1]
    nc_real = (seq - CMP_LEN) // CMP_STRIDE + 1
    ovt = _overlap_matrix(nc_pad, nc_real, n_sel, nsp).T
    q_t = qn.reshape(bsz, seq, H_NSA, HEAD_DIM).transpose(0, 2, 3, 1)
    ks, vst = _split_kv_t(kvs_bf, bsz, seq, KV_NSA)
    kw, vwt = _split_kv_t(kvw_bf, bsz, seq, KV_NSA)
    n_gate = 3 * GRP_NSA
    gate_rows = -(-n_gate // 8) * 8
    gates_t = jnp.pad(gn.reshape(bsz, seq, KV_NSA, n_gate).transpose(0, 2, 3, 1),
                      ((0, 0), (0, 0), (0, gate_rows - n_gate), (0, 0)))
    gw = GRP_NSA * HEAD_DIM
    w = GRP_NSA * tq
    body = functools.partial(_nsa_prompt_body, tq=tq, n_sel=n_sel, n_top=n_top)
    kv_spec = pl.BlockSpec((1, 1, seq, HEAD_DIM), lambda b, g, i: (b, g, 0, 0))
    kvt_spec = pl.BlockSpec((1, 1, HEAD_DIM, seq), lambda b, g, i: (b, g, 0, 0))
    return pl.pallas_call(
        body,
        grid=(bsz, KV_NSA, seq // tq),
        in_specs=[pl.BlockSpec((1, GRP_NSA, HEAD_DIM, tq), lambda b, g, i: (b, g, 0, i)),
                  pl.BlockSpec((1, nc_pad, 2 * HEAD_DIM), lambda b, g, i: (b, 0, g)),
                  kv_spec, kvt_spec, kv_spec, kvt_spec,
                  pl.BlockSpec((1, 1, gate_rows, tq), lambda b, g, i: (b, g, 0, i)),
                  pl.BlockSpec((nsp, nc_pad), lambda b, g, i: (0, 0))],
        out_specs=pl.BlockSpec((1, tq, gw), lambda b, g, i: (b, i, g)),
        out_shape=jax.ShapeDtypeStruct((bsz, seq, H_NSA * HEAD_DIM), BF16),
        scratch_shapes=[pltpu.VMEM((nsp, tq), F32),
                        pltpu.VMEM((1, w), F32),
                        pltpu.VMEM((1, w), F32),
                        pltpu.VMEM((HEAD_DIM, w), F32)],
        compiler_params=_cparams(("parallel", "parallel", "arbitrary")),
        name="nsa_prompt",
    )(q_t, kvc, ks, vst, kw, vwt, gates_t, ovt)


def _ordered_key(x):
    bits = pltpu.bitcast(x + 0.0, I32)
    return jnp.where(bits >= 0, bits, bits ^ jnp.int32(0x7FFFFFFF))


def _radix_kth(count_ge, shape, k):
    zero = jnp.zeros(shape, I32)
    p = jnp.where(count_ge(zero) >= k, zero, jnp.full(shape, INT_MIN, I32))

    def bit_step(i, p):
        cand = p + lax.shift_left(jnp.int32(1), jnp.int32(30) - i)
        return jnp.where(count_ge(cand) >= k, cand, p)

    return lax.fori_loop(0, 31, bit_step, p)


def _tie_cutoff(count_eq_below, need, shape, nbits):
    def bit_step(i, c):
        cand = c + lax.shift_left(jnp.int32(1), jnp.int32(nbits - 1) - i)
        return jnp.where(count_eq_below(cand) < need, cand, c)

    return lax.fori_loop(0, nbits, bit_step, jnp.zeros(shape, I32))


def _dsa_prompt_body(qd_ref, kd_ref, vdt_ref, qit_ref, wit_ref, ki2_ref, o_ref,
                     key_scr, neg_scr, cut_scr, m_scr, l_scr, acc_scr, *, tq, seq, topk):
    qi = pl.program_id(1)
    q0 = qi * tq
    nkt = (q0 + tq - 1) // K_TILE + 1
    qpos = q0 + lax.broadcasted_iota(I32, (1, tq), 1)
    wrow = wit_ref[0] * IDX_SCALE
    pad = jnp.zeros((LANES - IDX_DIM, tq), BF16)
    q_idx = [jnp.concatenate([qit_ref[0, h], pad], axis=0) for h in range(IDX_HEADS)]

    def tile_pos(kt):
        return kt * K_TILE + lax.broadcasted_iota(I32, (K_TILE, 1), 0)

    def key_rows(kt):
        return pl.ds(pl.multiple_of(kt * K_TILE, K_TILE), K_TILE)

    def score_step(kt, carry):
        k2 = ki2_ref[0, key_rows(kt), :].astype(BF16)
        acc = jnp.zeros((K_TILE, tq), F32)
        for h in range(0, IDX_HEADS, 2):
            sc = _dot(k2, jnp.concatenate([q_idx[h], q_idx[h + 1]], axis=1))
            acc = (acc + jnp.maximum(sc[:, :tq], 0.0) * wrow[h:h + 1]
                   + jnp.maximum(sc[:, tq:], 0.0) * wrow[h + 1:h + 2])
        key_scr[kt] = _ordered_key(jnp.where(tile_pos(kt) <= qpos, acc, NEG_INF))
        return carry

    lax.fori_loop(0, nkt, score_step, 0)

    def count(flag_fn):
        def step(kt, acc):
            return acc + flag_fn(kt)
        acc = lax.fori_loop(0, nkt, step, jnp.zeros((K_TILE, tq), F32))
        return jnp.sum(acc, axis=0, keepdims=True)

    thr = _radix_kth(lambda cand: count(lambda kt: jnp.where(key_scr[kt] >= cand, 1.0, 0.0)), (1, tq), topk)
    n_ge = count(lambda kt: jnp.where(key_scr[kt] >= thr, 1.0, 0.0))
    n_gt = count(lambda kt: jnp.where(key_scr[kt] > thr, 1.0, 0.0))
    cut_scr[...] = jnp.full((1, tq), seq, I32)

    @pl.when(jnp.max(n_ge) > topk)
    def _():
        def count_eq_below(c):
            return count(lambda kt: jnp.where(key_scr[kt] == thr, jnp.where(tile_pos(kt) < c, 1.0, 0.0), 0.0))
        cut_scr[...] = _tie_cutoff(count_eq_below, topk - n_gt, (1, tq), int(seq).bit_length())

    cut = cut_scr[...]

    def mask_step(kt, carry):
        key = key_scr[kt]
        t = tile_pos(kt)
        chosen = jnp.where(key > thr, 0.0, jnp.where(key == thr, jnp.where(t <= cut, 0.0, -jnp.inf), -jnp.inf))
        neg_scr[kt] = jnp.where(t <= qpos, chosen, -jnp.inf)
        return carry

    lax.fori_loop(0, nkt, mask_step, 0)

    n_heads = DSA_GROUPS_PER_PASS * GRP_DSA
    for g0 in range(0, KV_DSA, DSA_GROUPS_PER_PASS):
        groups = range(g0, g0 + DSA_GROUPS_PER_PASS)
        q_t = jnp.concatenate([qd_ref[0, g0 * GRP_DSA + r] for r in range(n_heads)], axis=1)
        o_t = _flash_t(q_t, lambda kt, groups=groups: [kd_ref[0, g, key_rows(kt), :] for g in groups],
                       lambda kt, groups=groups: [vdt_ref[0, g, :, key_rows(kt)] for g in groups],
                       lambda kt: neg_scr[kt], 0, nkt, m_scr, l_scr, acc_scr)
        for r in range(n_heads):
            h = g0 * GRP_DSA + r
            o_ref[0, :, h * LANES:(h + 1) * LANES] = o_t[:, r * tq:(r + 1) * tq].T.astype(o_ref.dtype)


def _dsa_prompt(qd, kvd_bf, misc, bsz, seq):
    tq = Q_TILE
    assert seq % K_TILE == 0
    topk = min(DSA_TOPK_MAX, seq // 4)
    assert topk <= K_TILE
    n_q = IDX_HEADS * IDX_DIM
    qd_t = qd.reshape(bsz, seq, H_DSA, HEAD_DIM).transpose(0, 2, 3, 1)
    kd, vdt = _split_kv_t(kvd_bf, bsz, seq, KV_DSA)
    qi_t = misc[:, :n_q].astype(BF16).reshape(bsz, seq, IDX_HEADS, IDX_DIM).transpose(0, 2, 3, 1)
    wi_t = misc[:, n_q + IDX_DIM:n_q + IDX_DIM + IDX_HEADS].reshape(bsz, seq, IDX_HEADS).transpose(0, 2, 1)
    body = functools.partial(_dsa_prompt_body, tq=tq, seq=seq, topk=topk)
    nkt = seq // K_TILE
    w = DSA_GROUPS_PER_PASS * GRP_DSA * tq
    return pl.pallas_call(
        body,
        grid=(bsz, seq // tq),
        in_specs=[pl.BlockSpec((1, H_DSA, HEAD_DIM, tq), lambda b, i: (b, 0, 0, i)),
                  pl.BlockSpec((1, KV_DSA, seq, HEAD_DIM), lambda b, i: (b, 0, 0, 0)),
                  pl.BlockSpec((1, KV_DSA, HEAD_DIM, seq), lambda b, i: (b, 0, 0, 0)),
                  pl.BlockSpec((1, IDX_HEADS, IDX_DIM, tq), lambda b, i: (b, 0, 0, i)),
                  pl.BlockSpec((1, IDX_HEADS, tq), lambda b, i: (b, 0, i)),
                  pl.BlockSpec((1, seq, LANES), lambda b, i: (b, 0, MISC_W // LANES - 1))],
        out_specs=pl.BlockSpec((1, tq, H_DSA * HEAD_DIM), lambda b, i: (b, i, 0)),
        out_shape=jax.ShapeDtypeStruct((bsz, seq, H_DSA * HEAD_DIM), BF16),
        scratch_shapes=[pltpu.VMEM((nkt, K_TILE, tq), I32),
                        pltpu.VMEM((nkt, K_TILE, tq), F32),
                        pltpu.VMEM((1, tq), I32),
                        pltpu.VMEM((1, w), F32),
                        pltpu.VMEM((1, w), F32),
                        pltpu.VMEM((HEAD_DIM, w), F32)],
        compiler_params=_cparams(("parallel", "arbitrary")),
        name="dsa_prompt",
    )(qd_t, kd, vdt, qi_t, wi_t, misc.reshape(bsz, seq, MISC_W))


def _bf16_round(x):
    return x.astype(BF16).astype(F32)


def _nsa_sample_cmp_body(q_ref, kvc_ref, ov_ref, oc_ref, sc_ref, *, qpos, n_sel):
    sc_ref[...] = jnp.zeros(sc_ref.shape, F32)
    rep = GRP_NSA
    for g in range(KV_NSA):
        q = q_ref[0, g * rep:(g + 1) * rep, :]
        kc = kvc_ref[0, :, 2 * g * HEAD_DIM:(2 * g + 1) * HEAD_DIM].astype(BF16)
        vc = kvc_ref[0, :, (2 * g + 1) * HEAD_DIM:(2 * g + 2) * HEAD_DIM].astype(BF16)
        s = _dot_t(q, kc) * ATTN_SCALE
        blk = lax.broadcasted_iota(I32, (1, kc.shape[0]), 1)
        pc = _masked_softmax(s, (blk * CMP_STRIDE + (CMP_LEN - 1)) <= qpos)
        oc_ref[0, g * rep:(g + 1) * rep, :] = _dot(pc.astype(BF16), vc)
        psum = jnp.sum(pc, axis=0, keepdims=True)
        imp = _importance(psum, ov_ref)
        sc_ref[0, g:g + 1, :] = _selection_scores(imp, jnp.full((1, 1), qpos, I32), n_sel)


def _topk_index_body(s_ref, o_ref, *, n_top):
    s = s_ref[...]
    lane = lax.broadcasted_iota(I32, (1, s.shape[1]), 1).astype(F32)
    olane = lax.broadcasted_iota(I32, (1, o_ref.shape[1]), 1)
    out = jnp.zeros(o_ref.shape, F32)
    for i in range(n_top):
        m = jnp.max(s, axis=-1, keepdims=True)
        idx = jnp.min(jnp.where(s == m, lane, 1e9), axis=-1, keepdims=True)
        out = jnp.where(olane == i, idx, out)
        s = jnp.where(lane == idx, -jnp.inf, s)
    o_ref[...] = out.astype(I32)


def _softmax_with_new(s, s_new):
    m = jnp.maximum(jnp.max(s, axis=-1, keepdims=True), s_new)
    m = jnp.where(m == -jnp.inf, 0.0, m)
    p = jnp.exp(s - m)
    p_new = jnp.exp(s_new - m)
    den = jnp.sum(p, axis=-1, keepdims=True) + p_new
    inv = jnp.where(den > 0.0, 1.0 / den, 0.0)
    return p * inv, p_new * inv


def _nsa_sample_attn_body(sel_ref, pt_ref, q_ref, selv_ref, new_ref, gate_ref, oc_ref, win_ref, ex_ref, pool_hbm, o_ref,
                          buf, sem, *, npages, n_top, n_past_blk):
    b = pl.program_id(0)
    rep = GRP_NSA
    gc_rows = KV_NSA * 2
    blk_rows = SEL_LEN * gc_rows
    blk_per_page = PAGE_SIZE // SEL_LEN
    n_blk = KV_NSA * n_top

    def blk_copy(i):
        j = jnp.minimum(sel_ref[b * n_blk + i], n_past_blk - 1)
        row = (pt_ref[b * npages + _div_pow2(j, blk_per_page)] * blk_per_page + _mod_pow2(j, blk_per_page)) * blk_rows
        return pltpu.make_async_copy(pool_hbm.at[pl.ds(pl.multiple_of(row, blk_rows), blk_rows)],
                                     buf.at[pl.ds(pl.multiple_of(i * blk_rows, blk_rows), blk_rows)], sem)

    def start(i, c):
        blk_copy(i).start()
        return c

    lax.fori_loop(0, n_blk, start, 0)

    def wait(i, c):
        blk_copy(i).wait()
        return c

    lax.fori_loop(0, n_blk, wait, 0)

    lane = lax.broadcasted_iota(I32, (1, LANES), 1)
    gate = gate_ref[0]
    for g in range(KV_NSA):
        rows = slice(g * rep, (g + 1) * rep)
        q = q_ref[0, rows, :]
        selv = selv_ref[0, g:g + 1, :]
        listed = lane < n_top
        past_blk = jnp.where(listed, jnp.where(selv < n_past_blk, 1.0, 0.0), 0.0)
        new_blk = jnp.where(listed, jnp.where(selv >= n_past_blk, 1.0, 0.0), 0.0)
        neg = jnp.where(_dot(past_blk.astype(BF16), ex_ref[...]) > 0.5, 0.0, -jnp.inf)
        new_neg = jnp.where(jnp.max(new_blk, axis=-1, keepdims=True) > 0.5, 0.0, -jnp.inf)
        base = g * n_top * blk_rows
        k = buf[pl.ds(base + 2 * g, n_top * SEL_LEN, stride=gc_rows), :].astype(BF16)
        v = buf[pl.ds(base + 2 * g + 1, n_top * SEL_LEN, stride=gc_rows), :].astype(BF16)
        k_new = _bf16_round(new_ref[0, :, 2 * g * HEAD_DIM:(2 * g + 1) * HEAD_DIM])
        v_new = _bf16_round(new_ref[0, :, (2 * g + 1) * HEAD_DIM:(2 * g + 2) * HEAD_DIM])
        s = _dot_t(q, k) * ATTN_SCALE + neg
        s_new = jnp.sum(q.astype(F32) * k_new, axis=-1, keepdims=True) * ATTN_SCALE + new_neg
        p, p_new = _softmax_with_new(s, s_new)
        o_s = _dot(p.astype(BF16), v) + _bf16_round(p_new) * v_new

        kw = win_ref[0, :, 2 * g * HEAD_DIM:(2 * g + 1) * HEAD_DIM].astype(BF16)
        vw = win_ref[0, :, (2 * g + 1) * HEAD_DIM:(2 * g + 2) * HEAD_DIM].astype(BF16)
        sw = _dot_t(q, kw) * ATTN_SCALE
        pw = jnp.exp(sw - jnp.max(sw, axis=-1, keepdims=True))
        pw = pw * (1.0 / jnp.sum(pw, axis=-1, keepdims=True))
        o_w = _dot(pw.astype(BF16), vw)
        gt = gate[rows, :]
        o = gt[:, 0:1] * oc_ref[0, rows, :] + gt[:, 1:2] * o_s + gt[:, 2:3] * o_w
        o_ref[0, rows, :] = o.astype(o_ref.dtype)


def _dsa_sample_score_body(pt_ref, q_ref, w_ref, kn_ref, pool_hbm, o_ref, buf, sem, *, npages):
    b = pl.program_id(0)

    def page_copy(p):
        return pltpu.make_async_copy(pool_hbm.at[pt_ref[b * npages + p]], buf.at[p], sem)

    def start(p, c):
        page_copy(p).start()
        return c

    lax.fori_loop(0, npages, start, 0)
    buf[npages] = kn_ref[0]

    def wait(p, c):
        page_copy(p).wait()
        return c

    lax.fori_loop(0, npages, wait, 0)
    keys = buf[...].reshape((npages + 1) * PAGE_SIZE, IDX_DIM).astype(BF16)
    sc = _dot_t(q_ref[0].astype(BF16), keys)
    o_ref[0] = jnp.sum(jnp.maximum(sc, 0.0) * (w_ref[0] * IDX_SCALE), axis=0, keepdims=True)


def _dsa_sample_select_body(s_ref, tri_ref, o_ref, cut_scr, rank_scr, *, length, topk):
    x = s_ref[...]
    nrow, width = x.shape
    shape = (nrow, 1)
    t = lax.broadcasted_iota(I32, (1, width), 1)
    key = _ordered_key(jnp.where(t < length, x, BELOW_NEG_INF))

    def count(flags):
        return jnp.sum(flags, axis=-1, keepdims=True)

    thr = _radix_kth(lambda cand: count(jnp.where(key >= cand, 1.0, 0.0)), shape, topk)
    n_ge = count(jnp.where(key >= thr, 1.0, 0.0))
    n_gt = count(jnp.where(key > thr, 1.0, 0.0))
    cut_scr[...] = jnp.full(shape, width, I32)

    @pl.when(jnp.max(n_ge) > topk)
    def _():
        cut_scr[...] = _tie_cutoff(
            lambda c: count(jnp.where(key == thr, jnp.where(t < c, 1.0, 0.0), 0.0)), topk - n_gt, shape,
            int(width).bit_length())

    cut = cut_scr[...]
    chosen = jnp.where(key > thr, 1.0, jnp.where(key == thr, jnp.where(t <= cut, 1.0, 0.0), 0.0))

    tri = tri_ref[...]
    ones = jnp.ones((LANES, LANES), BF16)
    before = jnp.zeros((nrow, LANES), F32)
    n_chunk = width // LANES
    for c in range(n_chunk):
        mc = chosen[:, c * LANES:(c + 1) * LANES]
        mcb = mc.astype(BF16)
        rank_scr[:, c * LANES:(c + 1) * LANES] = jnp.where(mc > 0.5, _dot(mcb, tri) + before - 1.0, -1.0)
        before = before + _dot(mcb, ones)

    wanted = lax.broadcasted_iota(I32, (topk, LANES), 0).astype(F32)
    lane = lax.broadcasted_iota(I32, (1, LANES), 1)

    def per_row(b, carry):
        ranks = rank_scr[pl.ds(b, 1), :]
        acc = jnp.zeros((topk, LANES), F32)
        for c in range(n_chunk):
            pos = (lane + c * LANES).astype(F32)
            acc = acc + jnp.where(ranks[:, c * LANES:(c + 1) * LANES] == wanted, pos, 0.0)
        o_ref[pl.ds(b, 1), :] = jnp.sum(acc.T, axis=0, keepdims=True).astype(I32)
        return carry

    lax.fori_loop(0, nrow, per_row, 0)


def _dsa_sample_attn_body(idx_ref, pt_ref, q_ref, idxv_ref, new_ref, pool_hbm, o_ref, buf, sem, *, npages, topk, past):
    b = pl.program_id(0)
    rep = GRP_DSA
    gc_rows = KV_DSA * 2

    def row_copy(r):
        t = jnp.minimum(idx_ref[b * topk + r], past - 1)
        row = (pt_ref[b * npages + _div_pow2(t, PAGE_SIZE)] * PAGE_SIZE + _mod_pow2(t, PAGE_SIZE)) * gc_rows
        return pltpu.make_async_copy(pool_hbm.at[pl.ds(pl.multiple_of(row, gc_rows), gc_rows)],
                                     buf.at[pl.ds(pl.multiple_of(r * gc_rows, gc_rows), gc_rows)], sem)

    def start(r, c):
        row_copy(r).start()
        return c

    lax.fori_loop(0, topk, start, 0)

    def wait(r, c):
        row_copy(r).wait()
        return c

    lax.fori_loop(0, topk, wait, 0)

    idxv = idxv_ref[0]
    neg = jnp.where(idxv < past, 0.0, -jnp.inf)
    new_neg = jnp.where(jnp.max(jnp.where(idxv >= past, 1.0, 0.0), axis=-1, keepdims=True) > 0.5, 0.0, -jnp.inf)
    for g in range(KV_DSA):
        rows = slice(g * rep, (g + 1) * rep)
        q = q_ref[0, rows, :]
        k = buf[pl.ds(2 * g, topk, stride=gc_rows), :].astype(BF16)
        v = buf[pl.ds(2 * g + 1, topk, stride=gc_rows), :].astype(BF16)
        k_new = _bf16_round(new_ref[0, :, 2 * g * HEAD_DIM:(2 * g + 1) * HEAD_DIM])
        v_new = _bf16_round(new_ref[0, :, (2 * g + 1) * HEAD_DIM:(2 * g + 2) * HEAD_DIM])
        s = _dot_t(q, k) * ATTN_SCALE + neg
        s_new = jnp.sum(q.astype(F32) * k_new, axis=-1, keepdims=True) * ATTN_SCALE + new_neg
        p, p_new = _softmax_with_new(s, s_new)
        o_ref[0, rows, :] = (_dot(p.astype(BF16), v) + _bf16_round(p_new) * v_new).astype(o_ref.dtype)


def _rope_tables(pos, dim):
    half = dim // 2
    inv = ROPE_THETA ** (-jnp.arange(half, dtype=F32) / half)
    ang = pos.astype(F32)[:, None] * inv[None, :]
    cos, sin = jnp.cos(ang), jnp.sin(ang)
    reps = LANES // dim
    return (jnp.tile(jnp.concatenate([cos, cos], axis=-1), (1, reps)),
            jnp.tile(jnp.concatenate([-sin, sin], axis=-1), (1, reps)))


def _project(h, w, pos, tm):
    d = h.shape[1]
    splits = (H_NSA * HEAD_DIM, KV_NSA * 2 * HEAD_DIM, KV_NSA * 2 * HEAD_DIM, KV_NSA * 2 * HEAD_DIM, 3 * H_NSA,
              H_DSA * HEAD_DIM, KV_DSA * 2 * HEAD_DIM, IDX_HEADS * IDX_DIM, IDX_DIM, IDX_HEADS, 2 * d)
    offs = np.concatenate([[0], np.cumsum(splits)])
    seg = lambda i: w[:, int(offs[i]):int(offs[i + 1])].astype(BF16)
    w_qn, w_kvc, w_kvs, w_kvw, w_gn, w_qd, w_kvd, w_qi, w_ki, w_wi, w_gm = (seg(i) for i in range(11))
    w_misc = jnp.concatenate([w_qi, w_ki, w_wi, w_gn, w_ki, w_ki], axis=1)
    assert w_misc.shape[1] == MISC_W
    cos128, sin128 = _rope_tables(pos, HEAD_DIM)
    cos64, sin64 = _rope_tables(pos, IDX_DIM)
    rope_all = lambda n: jnp.ones((1, n), F32)
    rope_k = lambda n: jnp.tile(jnp.concatenate([jnp.ones((1, HEAD_DIM), F32), jnp.zeros((1, HEAD_DIM), F32)], 1),
                                (1, n // (2 * HEAD_DIM)))

    def rope_mm(wseg, flag, dtypes, name):
        return _mm([h], [wseg], [(cos128, "row"), (sin128, "row"), (flag, "col")], _epi_rope128, dtypes,
                   tm=tm, tn=512, name=name)

    out = {}
    out["qn"], = rope_mm(w_qn, rope_all(w_qn.shape[1]), [BF16], "proj_qn")
    out["kvc"], = rope_mm(w_kvc, rope_k(w_kvc.shape[1]), [F32], "proj_kvc")
    out["kvs"], out["kvs_bf"] = rope_mm(w_kvs, rope_k(w_kvs.shape[1]), [F32, BF16], "proj_kvs")
    out["kvw"], out["kvw_bf"] = rope_mm(w_kvw, rope_k(w_kvw.shape[1]), [F32, BF16], "proj_kvw")
    out["qd"], = rope_mm(w_qd, rope_all(w_qd.shape[1]), [BF16], "proj_qd")
    out["kvd"], out["kvd_bf"] = rope_mm(w_kvd, rope_k(w_kvd.shape[1]), [F32, BF16], "proj_kvd")
    out["misc"], = _mm([h], [w_misc], [(cos64, "row"), (sin64, "row")], _epi_misc, [F32],
                       tm=min(tm, 512), tn=MISC_W, name="proj_misc")
    out["gm"], = _mm([h], [w_gm], [], _epi_sigmoid, [F32], tm=tm, tn=512, name="proj_gm")
    return out


def _merge_and_tail(x, o_n, o_d, gm, p, wts, tm):
    d = x.shape[1]
    tn = 512
    mix, = _mm([o_n, o_d], [wts["w_bn"], wts["w_bd"]], [(gm, ("tile", 0)), (gm, ("tile", d // tn))],
               _epi_merge, [BF16], tm=tm, tn=tn, name="merge")
    x1, = _mm([mix], [wts["w_out"]], [(x, ("tile", 0))], _epi_residual, [F32], tm=tm, tn=tn, name="out_proj")
    h2 = _rmsnorm(x1, wts["norm_mlp"], BF16)
    u, = _mm([h2], [wts["w_up"]], [], _epi_relu2, [BF16], tm=tm, tn=tn, name="mlp_up")
    x2, = _mm([u], [wts["w_down"]], [(x1, ("tile", 0))], _epi_residual, [F32], tm=tm, tn=1024, tk=2048,
              name="mlp_down")
    h3 = _rmsnorm(x2, wts["norm_ple"], BF16)
    x3, = _mm([h3, p], [wts["w_ple_gate"], wts["w_ple"]], [(x2, ("tile", 0))], _epi_ple, [F32],
              tm=tm, tn=tn, name="ple")
    return _rmsnorm(x3, wts["norm_final"], F32)


def kernel(x_prompt, x_sample, cache_nsa_cmp, cache_nsa_sel, cache_dsa_kv, cache_dsa_idx, state_nsa_win, page_table, p_prompt, p_sample, norm_mix, w_in, cmp_pos, cmp_w1, cmp_w2, w_branch_nsa, w_branch_dsa, w_out, norm_mlp, w_up, w_down, norm_ple, w_ple_gate, w_ple, norm_final):
    depth = w_in.shape[0]
    assert depth == 1
    bsz, seq, d = x_prompt.shape
    nb, dec_seq, _ = x_sample.shape
    assert dec_seq == 1
    npages = page_table.shape[1]
    past = npages * PAGE_SIZE
    length = past + dec_seq
    n_pool = cache_nsa_cmp.shape[1]
    gc_n = KV_NSA * 2
    gc_d = KV_DSA * 2

    wts = {
        "w_bn": w_branch_nsa[0].astype(BF16), "w_bd": w_branch_dsa[0].astype(BF16),
        "w_out": w_out[0].astype(BF16), "w_up": w_up[0].astype(BF16), "w_down": w_down[0].astype(BF16),
        "w_ple_gate": w_ple_gate[0].astype(BF16), "w_ple": w_ple[0].astype(BF16),
        "norm_mlp": norm_mlp[0], "norm_ple": norm_ple[0], "norm_final": norm_final,
    }
    w1 = cmp_w1[0].reshape(2, 2, CMP_STRIDE * HEAD_DIM, CMP_HID).astype(BF16)
    w2 = cmp_w2[0].astype(BF16)
    pos_emb = cmp_pos[0]

    mp = bsz * seq
    xp = x_prompt.reshape(mp, d)
    hp = _rmsnorm(xp, norm_mix[0], BF16)
    pr = _project(hp, w_in[0], jnp.arange(seq), tm=1024)

    chunk_rows = CMP_STRIDE * gc_n
    pages_p = seq * gc_n // (PAGE_SIZE * gc_n)
    kvc_p = _compress(pr["kvc"].reshape(mp * gc_n, HEAD_DIM),
                      jnp.arange(bsz * pages_p, dtype=I32).reshape(bsz, pages_p), PAGE_SIZE * gc_n, pos_emb, w1, w2)
    n_q = IDX_HEADS * IDX_DIM
    gn_p = pr["misc"][:, n_q + IDX_DIM + IDX_HEADS:n_q + IDX_DIM + IDX_HEADS + 3 * H_NSA]
    o_n_p = _nsa_prompt(pr["qn"], kvc_p, pr["kvs_bf"], pr["kvw_bf"], gn_p, bsz, seq)
    o_d_p = _dsa_prompt(pr["qd"], pr["kvd_bf"], pr["misc"], bsz, seq)
    y_p = _merge_and_tail(xp, o_n_p.reshape(mp, -1), o_d_p.reshape(mp, -1), pr["gm"],
                          p_prompt[0].reshape(mp, -1), wts, tm=1024)

    xs = x_sample.reshape(nb, d)
    hs = _rmsnorm(xs, norm_mix[0], BF16)
    sm = _project(hs, w_in[0], jnp.full((nb,), past, I32), tm=nb)
    misc_s = sm["misc"]
    pt_flat = page_table.reshape(-1).astype(I32)

    kvc_s = _compress(cache_nsa_cmp[0].reshape(n_pool * PAGE_SIZE * gc_n, HEAD_DIM), page_table,
                      PAGE_SIZE * gc_n, pos_emb, w1, w2)
    nc_s = kvc_s.shape[1]
    n_sel_s = -(-length // SEL_LEN)
    n_top_s = min(SEL_TOP, n_sel_s)
    sel_lanes = -(-n_sel_s // LANES) * LANES
    ov_s = _overlap_matrix(nc_s, (length - CMP_LEN) // CMP_STRIDE + 1, n_sel_s, sel_lanes)
    qn_s = sm["qn"].reshape(nb, H_NSA, HEAD_DIM)
    oc_s, score_s = pl.pallas_call(
        functools.partial(_nsa_sample_cmp_body, qpos=length - 1, n_sel=n_sel_s),
        grid=(nb,),
        in_specs=[pl.BlockSpec((1, H_NSA, HEAD_DIM), lambda b: (b, 0, 0)),
                  pl.BlockSpec((1, nc_s, gc_n * HEAD_DIM), lambda b: (b, 0, 0)),
                  pl.BlockSpec((nc_s, sel_lanes), lambda b: (0, 0))],
        out_specs=[pl.BlockSpec((1, H_NSA, HEAD_DIM), lambda b: (b, 0, 0)),
                   pl.BlockSpec((1, 8, sel_lanes), lambda b: (b, 0, 0))],
        out_shape=[jax.ShapeDtypeStruct((nb, H_NSA, HEAD_DIM), F32),
                   jax.ShapeDtypeStruct((nb, 8, sel_lanes), F32)],
        compiler_params=_cparams(("parallel",)),
        name="nsa_sample_cmp",
    )(qn_s, kvc_s, ov_s)
    assert n_top_s <= LANES
    sel_vec = pl.pallas_call(
        functools.partial(_topk_index_body, n_top=n_top_s),
        out_shape=jax.ShapeDtypeStruct((nb * KV_NSA, LANES), I32),
        compiler_params=pltpu.CompilerParams(vmem_limit_bytes=VMEM_LIMIT),
        name="nsa_sample_topk",
    )(score_s[:, :KV_NSA].reshape(nb * KV_NSA, sel_lanes))
    sel_idx = sel_vec[:, :n_top_s].reshape(-1)

    n_past_blk = past // SEL_LEN
    win_s = jnp.concatenate([state_nsa_win[0], sm["kvw"].reshape(nb, 1, KV_NSA, 2, HEAD_DIM)], axis=1)[:, dec_seq:]
    wb = win_s.shape[1]
    gn_s = misc_s[:, n_q + IDX_DIM + IDX_HEADS:n_q + IDX_DIM + IDX_HEADS + 3 * H_NSA]
    gates_s = jnp.pad(gn_s.reshape(nb, H_NSA, 3), ((0, 0), (0, 0), (0, LANES - 3)))
    n_gather = n_top_s * SEL_LEN
    expand = jnp.asarray((np.arange(n_gather)[None, :] // SEL_LEN == np.arange(LANES)[:, None]).astype(np.float32),
                         dtype=BF16)
    o_n_s = pl.pallas_call(
        functools.partial(_nsa_sample_attn_body, npages=npages, n_top=n_top_s, n_past_blk=n_past_blk),
        grid_spec=pltpu.PrefetchScalarGridSpec(
            num_scalar_prefetch=2,
            grid=(nb,),
            in_specs=[pl.BlockSpec((1, H_NSA, HEAD_DIM), lambda b, sel, pt: (b, 0, 0)),
                      pl.BlockSpec((1, KV_NSA, LANES), lambda b, sel, pt: (b, 0, 0)),
                      pl.BlockSpec((1, 1, gc_n * HEAD_DIM), lambda b, sel, pt: (b, 0, 0)),
                      pl.BlockSpec((1, H_NSA, LANES), lambda b, sel, pt: (b, 0, 0)),
                      pl.BlockSpec((1, H_NSA, HEAD_DIM), lambda b, sel, pt: (b, 0, 0)),
                      pl.BlockSpec((1, wb, gc_n * HEAD_DIM), lambda b, sel, pt: (b, 0, 0)),
                      pl.BlockSpec((LANES, n_gather), lambda b, sel, pt: (0, 0)),
                      pl.BlockSpec(memory_space=pl.ANY)],
            out_specs=pl.BlockSpec((1, H_NSA, HEAD_DIM), lambda b, sel, pt: (b, 0, 0)),
            scratch_shapes=[pltpu.VMEM((KV_NSA * n_gather * gc_n, HEAD_DIM), F32), pltpu.SemaphoreType.DMA(())]),
        out_shape=jax.ShapeDtypeStruct((nb, H_NSA, HEAD_DIM), BF16),
        compiler_params=_cparams(("arbitrary",)),
        name="nsa_sample_attn",
    )(sel_idx, pt_flat, qn_s, sel_vec.reshape(nb, KV_NSA, LANES), sm["kvs"].reshape(nb, 1, -1), gates_s, oc_s,
      win_s.reshape(nb, wb, gc_n * HEAD_DIM), expand, cache_nsa_sel[0].reshape(n_pool * PAGE_SIZE * gc_n, HEAD_DIM))

    qi_s = misc_s[:, :n_q].reshape(nb, IDX_HEADS, IDX_DIM)
    wi_s = misc_s[:, n_q + IDX_DIM:n_q + IDX_DIM + IDX_HEADS].reshape(nb, IDX_HEADS, 1)
    ki_new = jnp.pad(misc_s[:, n_q:n_q + IDX_DIM].reshape(nb, 1, IDX_DIM), ((0, 0), (0, PAGE_SIZE - 1), (0, 0)))
    width = (npages + 1) * PAGE_SIZE
    score_s = pl.pallas_call(
        functools.partial(_dsa_sample_score_body, npages=npages),
        grid_spec=pltpu.PrefetchScalarGridSpec(
            num_scalar_prefetch=1,
            grid=(nb,),
            in_specs=[pl.BlockSpec((1, IDX_HEADS, IDX_DIM), lambda b, pt: (b, 0, 0)),
                      pl.BlockSpec((1, IDX_HEADS, 1), lambda b, pt: (b, 0, 0)),
                      pl.BlockSpec((1, PAGE_SIZE, IDX_DIM), lambda b, pt: (b, 0, 0)),
                      pl.BlockSpec(memory_space=pl.ANY)],
            out_specs=pl.BlockSpec((1, 1, width), lambda b, pt: (b, 0, 0)),
            scratch_shapes=[pltpu.VMEM((npages + 1, PAGE_SIZE, IDX_DIM), F32), pltpu.SemaphoreType.DMA(())]),
        out_shape=jax.ShapeDtypeStruct((nb, 1, width), F32),
        compiler_params=_cparams(("arbitrary",)),
        name="dsa_sample_score",
    )(pt_flat, qi_s, wi_s, ki_new, cache_dsa_idx[0])
    topk_s = min(DSA_TOPK_MAX, length // 4)
    tri = jnp.asarray(np.triu(np.ones((LANES, LANES), np.float32)), dtype=BF16)
    top_idx = pl.pallas_call(
        functools.partial(_dsa_sample_select_body, length=length, topk=topk_s),
        out_shape=jax.ShapeDtypeStruct((nb, topk_s), I32),
        scratch_shapes=[pltpu.VMEM((nb, 1), I32), pltpu.VMEM((nb, width), F32)],
        compiler_params=pltpu.CompilerParams(vmem_limit_bytes=VMEM_LIMIT),
        name="dsa_sample_select",
    )(score_s.reshape(nb, width), tri)
    o_d_s = pl.pallas_call(
        functools.partial(_dsa_sample_attn_body, npages=npages, topk=topk_s, past=past),
        grid_spec=pltpu.PrefetchScalarGridSpec(
            num_scalar_prefetch=2,
            grid=(nb,),
            in_specs=[pl.BlockSpec((1, H_DSA, HEAD_DIM), lambda b, ix, pt: (b, 0, 0)),
                      pl.BlockSpec((1, 1, topk_s), lambda b, ix, pt: (b, 0, 0)),
                      pl.BlockSpec((1, 1, gc_d * HEAD_DIM), lambda b, ix, pt: (b, 0, 0)),
                      pl.BlockSpec(memory_space=pl.ANY)],
            out_specs=pl.BlockSpec((1, H_DSA, HEAD_DIM), lambda b, ix, pt: (b, 0, 0)),
            scratch_shapes=[pltpu.VMEM((topk_s * gc_d, HEAD_DIM), F32), pltpu.SemaphoreType.DMA(())]),
        out_shape=jax.ShapeDtypeStruct((nb, H_DSA, HEAD_DIM), BF16),
        compiler_params=_cparams(("arbitrary",)),
        name="dsa_sample_attn",
    )(top_idx.reshape(-1), pt_flat, sm["qd"].reshape(nb, H_DSA, HEAD_DIM), top_idx.reshape(nb, 1, topk_s),
      sm["kvd"].reshape(nb, 1, -1), cache_dsa_kv[0].reshape(n_pool * PAGE_SIZE * gc_d, HEAD_DIM))

    y_s = _merge_and_tail(xs, o_n_s.reshape(nb, -1), o_d_s.reshape(nb, -1), sm["gm"],
                          p_sample[0].reshape(nb, -1), wts, tm=nb)

    kv6 = lambda a, n, s, g: a.reshape(1, n, s, g, 2, HEAD_DIM)
    n_q_end = n_q + IDX_DIM
    return (y_p.reshape(bsz, seq, d), y_s.reshape(nb, dec_seq, d),
            kv6(pr["kvc"], bsz, seq, KV_NSA), kv6(sm["kvc"], nb, dec_seq, KV_NSA),
            kv6(pr["kvs"], bsz, seq, KV_NSA), kv6(sm["kvs"], nb, dec_seq, KV_NSA),
            kv6(pr["kvd"], bsz, seq, KV_DSA), kv6(sm["kvd"], nb, dec_seq, KV_DSA),
            pr["misc"][:, n_q:n_q_end].reshape(1, bsz, seq, IDX_DIM),
            misc_s[:, n_q:n_q_end].reshape(1, nb, dec_seq, IDX_DIM),
            kv6(pr["kvw"], bsz, seq, KV_NSA)[:, :, seq - min(WINDOW, seq):],
            win_s[None])
```

```python
import functools

import numpy as np
import jax
import jax.numpy as jnp
from jax import lax
from jax.experimental import pallas as pl
from jax.experimental.pallas import tpu as pltpu

F32 = jnp.float32
BF16 = jnp.bfloat16
I32 = jnp.int32

HEAD_DIM = 128
H_NSA = 16
H_DSA = 16
KV_NSA = 2
KV_DSA = 4
GRP_NSA = H_NSA // KV_NSA
GRP_DSA = H_DSA // KV_DSA
CMP_LEN = 32
CMP_STRIDE = 16
CMP_HID = 256
SEL_LEN = 64
SEL_TOP = 16
WINDOW = 512
IDX_HEADS = 16
IDX_DIM = 64
DSA_TOPK_MAX = 256
PAGE_SIZE = 128
ROPE_THETA = 10000.0
NORM_EPS = 1e-6
NEG_INF = -1e30
BELOW_NEG_INF = -3e38
FORCE_BONUS = 1e4

LANES = 128
Q_TILE = 128
K_TILE = 256
DSA_GROUPS_PER_PASS = 4
MISC_W = 1280
VMEM_LIMIT = 52 * 1024 * 1024
INT_MIN = -2 ** 31

ATTN_SCALE = HEAD_DIM ** -0.5
IDX_SCALE = (IDX_DIM ** -0.5) * (IDX_HEADS ** -0.5)
LOG2E = float(np.log2(np.e))
EXP2_SCALE = ATTN_SCALE * LOG2E


def _cparams(sem):
    return pltpu.CompilerParams(dimension_semantics=sem, vmem_limit_bytes=VMEM_LIMIT)


def _dot(a, b):
    return jnp.dot(a, b, preferred_element_type=F32)


def _dot_t(a, b):
    return lax.dot_general(a, b, (((1,), (1,)), ((), ())), preferred_element_type=F32)


def _sigmoid(x):
    return 1.0 / (1.0 + jnp.exp(-x))


def _div_pow2(x, n):
    assert n & (n - 1) == 0
    return lax.shift_right_logical(x, int(np.log2(n)))


def _mod_pow2(x, n):
    assert n & (n - 1) == 0
    return x & (n - 1)


def _rmsnorm_body(x_ref, g_ref, o_ref):
    x = x_ref[...]
    ms = jnp.mean(x * x, axis=-1, keepdims=True)
    o_ref[...] = (x * lax.rsqrt(ms + NORM_EPS) * g_ref[...]).astype(o_ref.dtype)


def _rmsnorm(x2d, g, out_dtype):
    m, d = x2d.shape
    tm = min(m, 256)
    return pl.pallas_call(
        _rmsnorm_body,
        grid=(m // tm,),
        in_specs=[pl.BlockSpec((tm, d), lambda i: (i, 0)),
                  pl.BlockSpec((1, d), lambda i: (0, 0))],
        out_specs=pl.BlockSpec((tm, d), lambda i: (i, 0)),
        out_shape=jax.ShapeDtypeStruct((m, d), out_dtype),
        compiler_params=_cparams(("parallel",)),
        name="rmsnorm",
    )(x2d, g.reshape(1, d))


def _mm_body(*refs, n_pairs, n_extra, n_out, nk, epi):
    a_refs = refs[:n_pairs]
    w_refs = refs[n_pairs:2 * n_pairs]
    e_refs = refs[2 * n_pairs:2 * n_pairs + n_extra]
    o_refs = refs[2 * n_pairs + n_extra:2 * n_pairs + n_extra + n_out]
    scratch = refs[2 * n_pairs + n_extra + n_out:]

    def dots():
        return [_dot(a[...].astype(BF16), w[...].astype(BF16)) for a, w in zip(a_refs, w_refs)]

    def finish(accs):
        val = epi(accs, e_refs)
        for o in o_refs:
            o[...] = val.astype(o.dtype)

    if nk == 1:
        finish(dots())
    else:
        acc_ref = scratch[0]
        k = pl.program_id(2)

        @pl.when(k == 0)
        def _():
            acc_ref[...] = jnp.zeros_like(acc_ref)

        acc_ref[...] += dots()[0]

        @pl.when(k == nk - 1)
        def _():
            finish([acc_ref[...]])


def _mm(a_list, w_list, extras, epi, out_dtypes, *, tm, tn, tk=None, name):
    m = a_list[0].shape[0]
    n = w_list[0].shape[1]
    tm = min(tm, m)
    tn = min(tn, n)
    assert m % tm == 0 and n % tn == 0
    nk = 1
    if tk is not None and tk < a_list[0].shape[1]:
        assert len(a_list) == 1 and a_list[0].shape[1] % tk == 0
        nk = a_list[0].shape[1] // tk
    in_specs = []
    for a in a_list:
        kk = a.shape[1]
        if nk == 1:
            in_specs.append(pl.BlockSpec((tm, kk), lambda i, j, k: (i, 0)))
        else:
            in_specs.append(pl.BlockSpec((tm, tk), lambda i, j, k: (i, k)))
    for w in w_list:
        kk = w.shape[0]
        if nk == 1:
            in_specs.append(pl.BlockSpec((kk, tn), lambda i, j, k: (0, j)))
        else:
            in_specs.append(pl.BlockSpec((tk, tn), lambda i, j, k: (k, j)))
    e_arrays = []
    for arr, kind in extras:
        e_arrays.append(arr)
        if kind == "row":
            nrow = arr.shape[0] // tm
            in_specs.append(pl.BlockSpec((tm, arr.shape[1]), lambda i, j, k, nrow=nrow: (i % nrow, 0)))
        elif kind == "col":
            in_specs.append(pl.BlockSpec((1, tn), lambda i, j, k: (0, j)))
        else:
            off = kind[1]
            in_specs.append(pl.BlockSpec((tm, tn), lambda i, j, k, off=off: (i, j + off)))
    out_specs = [pl.BlockSpec((tm, tn), lambda i, j, k: (i, j)) for _ in out_dtypes]
    out_shape = [jax.ShapeDtypeStruct((m, n), dt) for dt in out_dtypes]
    scratch = [pltpu.VMEM((tm, tn), F32)] if nk > 1 else []
    body = functools.partial(_mm_body, n_pairs=len(a_list), n_extra=len(extras),
                             n_out=len(out_dtypes), nk=nk, epi=epi)
    outs = pl.pallas_call(
        body,
        grid=(m // tm, n // tn, nk),
        in_specs=in_specs,
        out_specs=out_specs,
        out_shape=out_shape,
        scratch_shapes=scratch,
        compiler_params=_cparams(("parallel", "parallel", "arbitrary")),
        name=name,
    )(*a_list, *w_list, *e_arrays)
    return outs


def _rope128(a, cos, sin):
    return a * cos + pltpu.roll(a, HEAD_DIM // 2, 1) * sin


def _rope64(a, cos, sin, first_half):
    quarter = IDX_DIM // 2
    rot = jnp.where(first_half, pltpu.roll(a, LANES - quarter, 1), pltpu.roll(a, quarter, 1))
    return a * cos + rot * sin


def _epi_rope128(accs, e):
    acc = accs[0]
    cos, sin, flag = e[0][...], e[1][...], e[2][...]
    outs = []
    for h in range(acc.shape[1] // LANES):
        a = acc[:, h * LANES:(h + 1) * LANES]
        outs.append(jnp.where(flag[:, h * LANES:(h + 1) * LANES] > 0.5, _rope128(a, cos, sin), a))
    return jnp.concatenate(outs, axis=1)


def _epi_misc(accs, e):
    acc = accs[0]
    cos, sin = e[0][...], e[1][...]
    lane = lax.broadcasted_iota(I32, (1, LANES), 1)
    first_half = (lane % IDX_DIM) < (IDX_DIM // 2)
    n_q = IDX_HEADS * IDX_DIM // LANES
    outs = [_rope64(acc[:, c * LANES:(c + 1) * LANES], cos, sin, first_half) for c in range(n_q)]
    t = acc[:, n_q * LANES:(n_q + 1) * LANES]
    outs.append(jnp.where(lane < IDX_DIM, _rope64(t, cos, sin, first_half),
                          jnp.where(lane < IDX_DIM + IDX_HEADS, t, _sigmoid(t))))
    outs.append(_rope64(acc[:, (n_q + 1) * LANES:(n_q + 2) * LANES], cos, sin, first_half))
    return jnp.concatenate(outs, axis=1)


def _epi_sigmoid(accs, e):
    return _sigmoid(accs[0])


def _epi_merge(accs, e):
    return accs[0] * e[0][...] + accs[1] * e[1][...]


def _epi_residual(accs, e):
    return e[0][...] + accs[0]


def _epi_relu2(accs, e):
    return jnp.square(jnp.maximum(accs[0], 0.0))


def _epi_ple(accs, e):
    return e[0][...] + _sigmoid(accs[0]) * accs[1]


def _chunk_pitch(chunk_rows):
    return chunk_rows + 8


def _compress_body(pt_ref, rows_hbm, pos_ref, w1_ref, w2_ref, o_ref, buf, sems, *, npages, page_rows, nch, n_steps):
    s = pl.program_id(0)
    gc_rows = KV_NSA * 2
    chunk_rows = CMP_STRIDE * gc_rows
    chunks_per_page = page_rows // chunk_rows
    pitch = _chunk_pitch(chunk_rows)

    def chunk_copy(step, i):
        half = _mod_pow2(step, 2)
        ci = half * nch + i
        page = pt_ref[_div_pow2(step, 2) * npages + _div_pow2(ci, chunks_per_page)]
        src_row = page * page_rows + _mod_pow2(ci, chunks_per_page) * chunk_rows
        src = rows_hbm.at[pl.ds(pl.multiple_of(src_row, chunk_rows), chunk_rows)]
        dst = buf.at[half, pl.ds(pl.multiple_of(i * pitch, 8), chunk_rows)]
        return pltpu.make_async_copy(src, dst, sems.at[half])

    def for_each_copy(step, fn):
        def body(i, c):
            fn(chunk_copy(step, i))
            return c
        lax.fori_loop(0, nch + 1 - _mod_pow2(step, 2), body, 0)

    @pl.when(s == 0)
    def _():
        buf[1, pl.ds(nch * pitch, chunk_rows), :] = jnp.zeros((chunk_rows, LANES), F32)
        for_each_copy(s, lambda cp: cp.start())

    @pl.when(s + 1 < n_steps)
    def _():
        for_each_copy(s + 1, lambda cp: cp.start())

    for_each_copy(s, lambda cp: cp.wait())

    slot = _mod_pow2(s, 2)
    for gc in range(gc_rows):
        c = gc % 2
        xa = jnp.concatenate(
            [(buf[slot, pl.ds(gc_rows * l + gc, nch, stride=pitch), :] + pos_ref[c, l:l + 1, :]).astype(BF16)
             for l in range(CMP_STRIDE)], axis=1)
        xb = jnp.concatenate(
            [(buf[slot, pl.ds(gc_rows * l + gc + pitch, nch, stride=pitch), :]
              + pos_ref[c, CMP_STRIDE + l:CMP_STRIDE + l + 1, :]).astype(BF16)
             for l in range(CMP_STRIDE)], axis=1)
        hid = jnp.maximum(_dot(xa, w1_ref[c, 0]) + _dot(xb, w1_ref[c, 1]), 0.0)
        o_ref[0, :, gc * LANES:(gc + 1) * LANES] = _dot(hid.astype(BF16), w2_ref[c])


def _compress(rows, page_table, page_rows, cmp_pos, w1, w2):
    nseq, npages = page_table.shape
    gc_rows = KV_NSA * 2
    chunk_rows = CMP_STRIDE * gc_rows
    nc = npages * page_rows // chunk_rows
    assert nc % 16 == 0
    nch = nc // 2
    n_steps = 2 * nseq
    body = functools.partial(_compress_body, npages=npages, page_rows=page_rows, nch=nch, n_steps=n_steps)
    grid_spec = pltpu.PrefetchScalarGridSpec(
        num_scalar_prefetch=1,
        grid=(n_steps,),
        in_specs=[pl.BlockSpec(memory_space=pl.ANY),
                  pl.BlockSpec((2, CMP_LEN, HEAD_DIM), lambda s, pt: (0, 0, 0)),
                  pl.BlockSpec((2, 2, CMP_STRIDE * HEAD_DIM, CMP_HID), lambda s, pt: (0, 0, 0, 0)),
                  pl.BlockSpec((2, CMP_HID, HEAD_DIM), lambda s, pt: (0, 0, 0))],
        out_specs=pl.BlockSpec((1, nch, gc_rows * HEAD_DIM), lambda s, pt: (s // 2, s % 2, 0)),
        scratch_shapes=[pltpu.VMEM((2, (nch + 1) * _chunk_pitch(chunk_rows), LANES), F32),
                        pltpu.SemaphoreType.DMA((2,))],
    )
    return pl.pallas_call(
        body,
        grid_spec=grid_spec,
        out_shape=jax.ShapeDtypeStruct((nseq, nc, gc_rows * HEAD_DIM), F32),
        compiler_params=_cparams(("arbitrary",)),
        name="compress",
    )(page_table.reshape(-1).astype(I32), rows, cmp_pos, w1, w2)


def _softmax_cols(s):
    m = jnp.max(s, axis=0, keepdims=True)
    p = jnp.exp2((s - jnp.where(m == -jnp.inf, 0.0, m)) * LOG2E)
    den = jnp.sum(p, axis=0, keepdims=True)
    return p * jnp.where(den > 0.0, 1.0 / den, 0.0)


def _flash_t(q_t, k_fn, vt_fn, neg_fn, lo, hi, m_scr, l_scr, acc_scr):
    w = q_t.shape[1]
    chunk = 2 * Q_TILE
    m_scr[...] = jnp.full(m_scr.shape, -jnp.inf, F32)
    l_scr[...] = jnp.zeros(l_scr.shape, F32)
    acc_scr[...] = jnp.zeros(acc_scr.shape, F32)

    def step(kt, carry):
        ks = k_fn(kt)
        vts = vt_fn(kt)
        neg2 = [jnp.concatenate([neg, neg], axis=1) for neg in neg_fn(kt)]
        n_chunk = w // chunk
        per_group = n_chunk // len(ks)
        per_mask = n_chunk // len(neg2)
        cols = [slice(c * chunk, (c + 1) * chunk) for c in range(n_chunk)]
        m_old = m_scr[...]
        l_old = l_scr[...]
        s = [_dot(ks[c // per_group], q_t[:, cols[c]]) + neg2[c // per_mask] for c in range(n_chunk)]
        m_new = jnp.maximum(m_old, jnp.concatenate([jnp.max(sc, axis=0, keepdims=True) for sc in s], axis=1))
        m_ref = jnp.where(m_new == -jnp.inf, 0.0, m_new)
        alpha = jnp.exp2((m_old - m_ref) * EXP2_SCALE)
        p = [jnp.exp2((s[c] - m_ref[:, cols[c]]) * EXP2_SCALE) for c in range(n_chunk)]
        l_scr[...] = alpha * l_old + jnp.concatenate([jnp.sum(pc, axis=0, keepdims=True) for pc in p], axis=1)
        m_scr[...] = m_new
        pv = [_dot(vts[c // per_group], p[c].astype(BF16)) for c in range(n_chunk)]
        for c in range(n_chunk):
            acc_scr[:, cols[c]] = alpha[:, cols[c]] * acc_scr[:, cols[c]] + pv[c]
        return carry

    lax.fori_loop(lo, hi, step, 0)
    l = l_scr[...]
    return acc_scr[...] * jnp.where(l > 0.0, 1.0 / l, 0.0)


def _masked_softmax(s, keep):
    sm = jnp.where(keep, s, NEG_INF)
    m = jnp.max(sm, axis=-1, keepdims=True)
    p = jnp.where(keep, jnp.exp(sm - m), 0.0)
    den = jnp.sum(p, axis=-1, keepdims=True)
    return p * jnp.where(den > 0.0, 1.0 / den, 0.0)


def _importance(psum, ov_ref):
    hi = psum.astype(BF16)
    lo = (psum - hi.astype(F32)).astype(BF16)
    ov = ov_ref[...]
    return _dot(hi, ov) + _dot(lo, ov)


def _selection_scores(imp, qpos, n_sel):
    sb = lax.broadcasted_iota(I32, (1, imp.shape[1]), 1)
    cur = lax.shift_right_logical(qpos, int(np.log2(SEL_LEN)))
    forced = jnp.where(sb == 0, FORCE_BONUS,
                       jnp.where(sb == cur, FORCE_BONUS, jnp.where(sb == cur - 1, FORCE_BONUS, 0.0)))
    score = jnp.where(sb <= cur, imp + forced, NEG_INF)
    return jnp.where(sb < n_sel, score, BELOW_NEG_INF)


def _nsa_prompt_body(q_ref, kvc_ref, ks_ref, vst_ref, kw_ref, vwt_ref, gate_ref, ovt_ref, o_ref,
                     sel_scr, m_scr, l_scr, acc_scr, *, tq, n_sel, n_top):
    qi = pl.program_id(1)
    q0 = qi * tq
    rep = GRP_NSA
    gw = rep * tq
    chunk = 2 * tq
    q_t = jnp.concatenate([q_ref[0, h] for h in range(H_NSA)], axis=1)
    qpos = q0 + lax.broadcasted_iota(I32, (1, tq), 1)
    qpos2 = jnp.concatenate([qpos, qpos], axis=1)
    ncp = kvc_ref.shape[1]
    blk_end = lax.broadcasted_iota(I32, (ncp, 1), 0) * CMP_STRIDE + (CMP_LEN - 1)
    neg_c = jnp.where(blk_end <= qpos2, 0.0, -jnp.inf)
    ovt = ovt_ref[...]
    sb = lax.broadcasted_iota(I32, (ovt.shape[0], 1), 0)
    cur = lax.shift_right_logical(qpos, int(np.log2(SEL_LEN)))
    forced = jnp.where(sb == 0, FORCE_BONUS,
                       jnp.where(sb == cur, FORCE_BONUS, jnp.where(sb == cur - 1, FORCE_BONUS, 0.0)))

    oc_parts = []
    for g in range(KV_NSA):
        kc = kvc_ref[0, :, 2 * g * HEAD_DIM:(2 * g + 1) * HEAD_DIM].astype(BF16)
        vct = kvc_ref[0, :, (2 * g + 1) * HEAD_DIM:(2 * g + 2) * HEAD_DIM].T.astype(BF16)
        psum = jnp.zeros((ncp, tq), F32)
        for c in range(gw // chunk):
            lo_col = g * gw + c * chunk
            pc = _softmax_cols(_dot(kc, q_t[:, lo_col:lo_col + chunk]) * ATTN_SCALE + neg_c)
            oc_parts.append(_dot(vct, pc.astype(BF16)))
            psum = psum + pc[:, :tq] + pc[:, tq:]

        hi = psum.astype(BF16)
        lo = (psum - hi.astype(F32)).astype(BF16)
        imp = _dot(ovt, hi) + _dot(ovt, lo)
        score = jnp.where(sb < n_sel, jnp.where(sb <= cur, imp + forced, NEG_INF), BELOW_NEG_INF)
        rank = jnp.zeros(score.shape, F32)
        for j in range(n_sel):
            row = score[j:j + 1, :]
            rank = rank + jnp.where(row > score, 1.0, jnp.where(row == score, jnp.where(sb > j, 1.0, 0.0), 0.0))
        sel_scr[g] = jnp.where(rank < n_top, 0.0, -jnp.inf)
    o_ct = jnp.concatenate(oc_parts, axis=1)

    blocks_per_tile = K_TILE // SEL_LEN

    def sel_neg(kt):
        t = kt * K_TILE + lax.broadcasted_iota(I32, (K_TILE, 1), 0)
        out = []
        for g in range(KV_NSA):
            blk_neg = jnp.concatenate(
                [jnp.broadcast_to(sel_scr[g, pl.ds(kt * blocks_per_tile + c, 1), :], (SEL_LEN, tq))
                 for c in range(blocks_per_tile)], axis=0)
            out.append(jnp.where(t <= qpos, blk_neg, -jnp.inf))
        return out

    def tiles(ref, kt, size):
        return [ref[0, g, pl.ds(pl.multiple_of(kt * size, size), size), :] for g in range(KV_NSA)]

    def tiles_t(ref, kt, size):
        return [ref[0, g, :, pl.ds(pl.multiple_of(kt * size, size), size)] for g in range(KV_NSA)]

    o_st = _flash_t(q_t, lambda kt: tiles(ks_ref, kt, K_TILE), lambda kt: tiles_t(vst_ref, kt, K_TILE), sel_neg,
                    0, (q0 + tq - 1) // K_TILE + 1, m_scr, l_scr, acc_scr)

    def win_neg(kt):
        d = qpos - (kt * tq + lax.broadcasted_iota(I32, (tq, 1), 0))
        return [jnp.where(d >= 0, jnp.where(d < WINDOW, 0.0, -jnp.inf), -jnp.inf)]

    o_wt = _flash_t(q_t, lambda kt: tiles(kw_ref, kt, tq), lambda kt: tiles_t(vwt_ref, kt, tq), win_neg,
                    jnp.maximum(qi - WINDOW // tq, 0), qi + 1, m_scr, l_scr, acc_scr)

    for h in range(H_NSA):
        gate = gate_ref[0, h // rep]
        r = h % rep
        cols = slice(h * tq, (h + 1) * tq)
        o_t = (gate[3 * r:3 * r + 1] * o_ct[:, cols] + gate[3 * r + 1:3 * r + 2] * o_st[:, cols]
               + gate[3 * r + 2:3 * r + 3] * o_wt[:, cols])
        o_ref[0, :, h * LANES:(h + 1) * LANES] = o_t.T.astype(o_ref.dtype)


def _overlap_matrix(nc_pad, nc_real, n_sel, lanes):
    n = np.arange(nc_pad)[:, None]
    j = np.arange(lanes)[None, :]
    ov = ((n * CMP_STRIDE < (j + 1) * SEL_LEN) & (n * CMP_STRIDE + CMP_LEN > j * SEL_LEN)
          & (n < nc_real) & (j < n_sel))
    return jnp.asarray(ov.astype(np.float32), dtype=BF16)


def _split_kv_t(kv_bf, bsz, seq, groups):
    kv = kv_bf.reshape(bsz, seq, groups, 2, HEAD_DIM)
    return kv[:, :, :, 0].transpose(0, 2, 1, 3), kv[:, :, :, 1].transpose(0, 2, 3, 1)


def _nsa_prompt(qn, kvc, kvs_bf, kvw_bf, gn, bsz, seq):
    tq = Q_TILE
    assert seq % K_TILE == 0 and WINDOW % tq == 0
    n_sel = -(-seq // SEL_LEN)
    n_top = min(SEL_TOP, n_sel)
    nsp = -(-n_sel // 8) * 8
    nc_pad = kvc.shape[1]
    nc_real = (seq - CMP_LEN) // CMP_STRIDE + 1
    ovt = _overlap_matrix(nc_pad, nc_real, n_sel, nsp).T
    q_t = qn.reshape(bsz, seq, H_NSA, HEAD_DIM).transpose(0, 2, 3, 1)
    ks, vst = _split_kv_t(kvs_bf, bsz, seq, KV_NSA)
    kw, vwt = _split_kv_t(kvw_bf, bsz, seq, KV_NSA)
    n_gate = 3 * GRP_NSA
    gate_rows = -(-n_gate // 8) * 8
    gates_t = jnp.pad(gn.reshape(bsz, seq, KV_NSA, n_gate).transpose(0, 2, 3, 1),
                      ((0, 0), (0, 0), (0, gate_rows - n_gate), (0, 0)))
    w = H_NSA * tq
    body = functools.partial(_nsa_prompt_body, tq=tq, n_sel=n_sel, n_top=n_top)
    kv_spec = pl.BlockSpec((1, KV_NSA, seq, HEAD_DIM), lambda b, i: (b, 0, 0, 0))
    kvt_spec = pl.BlockSpec((1, KV_NSA, HEAD_DIM, seq), lambda b, i: (b, 0, 0, 0))
    return pl.pallas_call(
        body,
        grid=(bsz, seq // tq),
        in_specs=[pl.BlockSpec((1, H_NSA, HEAD_DIM, tq), lambda b, i: (b, 0, 0, i)),
                  pl.BlockSpec((1, nc_pad, KV_NSA * 2 * HEAD_DIM), lambda b, i: (b, 0, 0)),
                  kv_spec, kvt_spec, kv_spec, kvt_spec,
                  pl.BlockSpec((1, KV_NSA, gate_rows, tq), lambda b, i: (b, 0, 0, i)),
                  pl.BlockSpec((nsp, nc_pad), lambda b, i: (0, 0))],
        out_specs=pl.BlockSpec((1, tq, H_NSA * HEAD_DIM), lambda b, i: (b, i, 0)),
        out_shape=jax.ShapeDtypeStruct((bsz, seq, H_NSA * HEAD_DIM), BF16),
        scratch_shapes=[pltpu.VMEM((KV_NSA, nsp, tq), F32),
                        pltpu.VMEM((1, w), F32),
                        pltpu.VMEM((1, w), F32),
                        pltpu.VMEM((HEAD_DIM, w), F32)],
        compiler_params=_cparams(("parallel", "arbitrary")),
        name="nsa_prompt",
    )(q_t, kvc, ks, vst, kw, vwt, gates_t, ovt)


def _ordered_key(x):
    bits = pltpu.bitcast(x + 0.0, I32)
    return jnp.where(bits >= 0, bits, bits ^ jnp.int32(0x7FFFFFFF))


def _radix_kth(count_ge, shape, k):
    zero = jnp.zeros(shape, I32)
    p = jnp.where(count_ge(zero) >= k, zero, jnp.full(shape, INT_MIN, I32))

    def bit_step(i, p):
        cand = p + lax.shift_left(jnp.int32(1), jnp.int32(30) - i)
        return jnp.where(count_ge(cand) >= k, cand, p)

    return lax.fori_loop(0, 31, bit_step, p)


def _tie_cutoff(count_eq_below, need, shape, nbits):
    def bit_step(i, c):
        cand = c + lax.shift_left(jnp.int32(1), jnp.int32(nbits - 1) - i)
        return jnp.where(count_eq_below(cand) < need, cand, c)

    return lax.fori_loop(0, nbits, bit_step, jnp.zeros(shape, I32))


def _dsa_prompt_body(qd_ref, kd_ref, vdt_ref, qit_ref, wit_ref, ki2_ref, o_ref,
                     key_scr, neg_scr, cut_scr, m_scr, l_scr, acc_scr, *, tq, seq, topk):
    qi = pl.program_id(1)
    q0 = qi * tq
    nkt = (q0 + tq - 1) // K_TILE + 1
    qpos = q0 + lax.broadcasted_iota(I32, (1, tq), 1)
    wrow = wit_ref[0] * IDX_SCALE
    pad = jnp.zeros((LANES - IDX_DIM, tq), BF16)
    q_idx = [jnp.concatenate([qit_ref[0, h], pad], axis=0) for h in range(IDX_HEADS)]

    def tile_pos(kt):
        return kt * K_TILE + lax.broadcasted_iota(I32, (K_TILE, 1), 0)

    def key_rows(kt):
        return pl.ds(pl.multiple_of(kt * K_TILE, K_TILE), K_TILE)

    def score_step(kt, carry):
        k2 = ki2_ref[0, key_rows(kt), :].astype(BF16)
        acc = jnp.zeros((K_TILE, tq), F32)
        for h in range(0, IDX_HEADS, 2):
            sc = _dot(k2, jnp.concatenate([q_idx[h], q_idx[h + 1]], axis=1))
            acc = (acc + jnp.maximum(sc[:, :tq], 0.0) * wrow[h:h + 1]
                   + jnp.maximum(sc[:, tq:], 0.0) * wrow[h + 1:h + 2])
        key_scr[kt] = _ordered_key(jnp.where(tile_pos(kt) <= qpos, acc, NEG_INF))
        return carry

    lax.fori_loop(0, nkt, score_step, 0)

    def count(flag_fn):
        def step(kt, acc):
            return acc + flag_fn(kt)
        acc = lax.fori_loop(0, nkt, step, jnp.zeros((K_TILE, tq), F32))
        return jnp.sum(acc, axis=0, keepdims=True)

    thr = _radix_kth(lambda cand: count(lambda kt: jnp.where(key_scr[kt] >= cand, 1.0, 0.0)), (1, tq), topk)
    n_ge = count(lambda kt: jnp.where(key_scr[kt] >= thr, 1.0, 0.0))
    n_gt = count(lambda kt: jnp.where(key_scr[kt] > thr, 1.0, 0.0))
    cut_scr[...] = jnp.full((1, tq), seq, I32)

    @pl.when(jnp.max(n_ge) > topk)
    def _():
        def count_eq_below(c):
            return count(lambda kt: jnp.where(key_scr[kt] == thr, jnp.where(tile_pos(kt) < c, 1.0, 0.0), 0.0))
        cut_scr[...] = _tie_cutoff(count_eq_below, topk - n_gt, (1, tq), int(seq).bit_length())

    cut = cut_scr[...]

    def mask_step(kt, carry):
        key = key_scr[kt]
        t = tile_pos(kt)
        chosen = jnp.where(key > thr, 0.0, jnp.where(key == thr, jnp.where(t <= cut, 0.0, -jnp.inf), -jnp.inf))
        neg_scr[kt] = jnp.where(t <= qpos, chosen, -jnp.inf)
        return carry

    lax.fori_loop(0, nkt, mask_step, 0)

    n_heads = DSA_GROUPS_PER_PASS * GRP_DSA
    for g0 in range(0, KV_DSA, DSA_GROUPS_PER_PASS):
        groups = range(g0, g0 + DSA_GROUPS_PER_PASS)
        q_t = jnp.concatenate([qd_ref[0, g0 * GRP_DSA + r] for r in range(n_heads)], axis=1)
        o_t = _flash_t(q_t, lambda kt, groups=groups: [kd_ref[0, g, key_rows(kt), :] for g in groups],
                       lambda kt, groups=groups: [vdt_ref[0, g, :, key_rows(kt)] for g in groups],
                       lambda kt: [neg_scr[kt]], 0, nkt, m_scr, l_scr, acc_scr)
        for r in range(n_heads):
            h = g0 * GRP_DSA + r
            o_ref[0, :, h * LANES:(h + 1) * LANES] = o_t[:, r * tq:(r + 1) * tq].T.astype(o_ref.dtype)


def _dsa_prompt(qd, kvd_bf, misc, bsz, seq):
    tq = Q_TILE
    assert seq % K_TILE == 0
    topk = min(DSA_TOPK_MAX, seq // 4)
    assert topk <= K_TILE
    n_q = IDX_HEADS * IDX_DIM
    qd_t = qd.reshape(bsz, seq, H_DSA, HEAD_DIM).transpose(0, 2, 3, 1)
    kd, vdt = _split_kv_t(kvd_bf, bsz, seq, KV_DSA)
    qi_t = misc[:, :n_q].astype(BF16).reshape(bsz, seq, IDX_HEADS, IDX_DIM).transpose(0, 2, 3, 1)
    wi_t = misc[:, n_q + IDX_DIM:n_q + IDX_DIM + IDX_HEADS].reshape(bsz, seq, IDX_HEADS).transpose(0, 2, 1)
    body = functools.partial(_dsa_prompt_body, tq=tq, seq=seq, topk=topk)
    nkt = seq // K_TILE
    w = DSA_GROUPS_PER_PASS * GRP_DSA * tq
    return pl.pallas_call(
        body,
        grid=(bsz, seq // tq),
        in_specs=[pl.BlockSpec((1, H_DSA, HEAD_DIM, tq), lambda b, i: (b, 0, 0, i)),
                  pl.BlockSpec((1, KV_DSA, seq, HEAD_DIM), lambda b, i: (b, 0, 0, 0)),
                  pl.BlockSpec((1, KV_DSA, HEAD_DIM, seq), lambda b, i: (b, 0, 0, 0)),
                  pl.BlockSpec((1, IDX_HEADS, IDX_DIM, tq), lambda b, i: (b, 0, 0, i)),
                  pl.BlockSpec((1, IDX_HEADS, tq), lambda b, i: (b, 0, i)),
                  pl.BlockSpec((1, seq, LANES), lambda b, i: (b, 0, MISC_W // LANES - 1))],
        out_specs=pl.BlockSpec((1, tq, H_DSA * HEAD_DIM), lambda b, i: (b, i, 0)),
        out_shape=jax.ShapeDtypeStruct((bsz, seq, H_DSA * HEAD_DIM), BF16),
        scratch_shapes=[pltpu.VMEM((nkt, K_TILE, tq), I32),
                        pltpu.VMEM((nkt, K_TILE, tq), F32),
                        pltpu.VMEM((1, tq), I32),
                        pltpu.VMEM((1, w), F32),
                        pltpu.VMEM((1, w), F32),
                        pltpu.VMEM((HEAD_DIM, w), F32)],
        compiler_params=_cparams(("parallel", "arbitrary")),
        name="dsa_prompt",
    )(qd_t, kd, vdt, qi_t, wi_t, misc.reshape(bsz, seq, MISC_W))


def _bf16_round(x):
    return x.astype(BF16).astype(F32)


def _nsa_sample_cmp_body(q_ref, kvc_ref, ov_ref, oc_ref, sc_ref, *, qpos, n_sel):
    sc_ref[...] = jnp.zeros(sc_ref.shape, F32)
    rep = GRP_NSA
    for g in range(KV_NSA):
        q = q_ref[0, g * rep:(g + 1) * rep, :]
        kc = kvc_ref[0, :, 2 * g * HEAD_DIM:(2 * g + 1) * HEAD_DIM].astype(BF16)
        vc = kvc_ref[0, :, (2 * g + 1) * HEAD_DIM:(2 * g + 2) * HEAD_DIM].astype(BF16)
        s = _dot_t(q, kc) * ATTN_SCALE
        blk = lax.broadcasted_iota(I32, (1, kc.shape[0]), 1)
        pc = _masked_softmax(s, (blk * CMP_STRIDE + (CMP_LEN - 1)) <= qpos)
        oc_ref[0, g * rep:(g + 1) * rep, :] = _dot(pc.astype(BF16), vc)
        psum = jnp.sum(pc, axis=0, keepdims=True)
        imp = _importance(psum, ov_ref)
        sc_ref[0, g:g + 1, :] = _selection_scores(imp, jnp.full((1, 1), qpos, I32), n_sel)


def _topk_index_body(s_ref, o_ref, *, n_top):
    s = s_ref[...]
    lane = lax.broadcasted_iota(I32, (1, s.shape[1]), 1).astype(F32)
    olane = lax.broadcasted_iota(I32, (1, o_ref.shape[1]), 1)
    out = jnp.zeros(o_ref.shape, F32)
    for i in range(n_top):
        m = jnp.max(s, axis=-1, keepdims=True)
        idx = jnp.min(jnp.where(s == m, lane, 1e9), axis=-1, keepdims=True)
        out = jnp.where(olane == i, idx, out)
        s = jnp.where(lane == idx, -jnp.inf, s)
    o_ref[...] = out.astype(I32)


def _softmax_with_new(s, s_new):
    m = jnp.maximum(jnp.max(s, axis=-1, keepdims=True), s_new)
    m = jnp.where(m == -jnp.inf, 0.0, m)
    p = jnp.exp(s - m)
    p_new = jnp.exp(s_new - m)
    den = jnp.sum(p, axis=-1, keepdims=True) + p_new
    inv = jnp.where(den > 0.0, 1.0 / den, 0.0)
    return p * inv, p_new * inv


def _nsa_sample_attn_body(sel_ref, pt_ref, q_ref, selv_ref, new_ref, gate_ref, oc_ref, win_ref, ex_ref, pool_hbm, o_ref,
                          buf, sem, *, npages, n_top, n_past_blk):
    b = pl.program_id(0)
    rep = GRP_NSA
    gc_rows = KV_NSA * 2
    blk_rows = SEL_LEN * gc_rows
    blk_per_page = PAGE_SIZE // SEL_LEN
    n_blk = KV_NSA * n_top

    def blk_copy(i):
        j = jnp.minimum(sel_ref[b * n_blk + i], n_past_blk - 1)
        row = (pt_ref[b * npages + _div_pow2(j, blk_per_page)] * blk_per_page + _mod_pow2(j, blk_per_page)) * blk_rows
        return pltpu.make_async_copy(pool_hbm.at[pl.ds(pl.multiple_of(row, blk_rows), blk_rows)],
                                     buf.at[pl.ds(pl.multiple_of(i * blk_rows, blk_rows), blk_rows)], sem)

    def start(i, c):
        blk_copy(i).start()
        return c

    lax.fori_loop(0, n_blk, start, 0)

    def wait(i, c):
        blk_copy(i).wait()
        return c

    lax.fori_loop(0, n_blk, wait, 0)

    lane = lax.broadcasted_iota(I32, (1, LANES), 1)
    gate = gate_ref[0]
    for g in range(KV_NSA):
        rows = slice(g * rep, (g + 1) * rep)
        q = q_ref[0, rows, :]
        selv = selv_ref[0, g:g + 1, :]
        listed = lane < n_top
        past_blk = jnp.where(listed, jnp.where(selv < n_past_blk, 1.0, 0.0), 0.0)
        new_blk = jnp.where(listed, jnp.where(selv >= n_past_blk, 1.0, 0.0), 0.0)
        neg = jnp.where(_dot(past_blk.astype(BF16), ex_ref[...]) > 0.5, 0.0, -jnp.inf)
        new_neg = jnp.where(jnp.max(new_blk, axis=-1, keepdims=True) > 0.5, 0.0, -jnp.inf)
        base = g * n_top * blk_rows
        k = buf[pl.ds(base + 2 * g, n_top * SEL_LEN, stride=gc_rows), :].astype(BF16)
        v = buf[pl.ds(base + 2 * g + 1, n_top * SEL_LEN, stride=gc_rows), :].astype(BF16)
        k_new = _bf16_round(new_ref[0, :, 2 * g * HEAD_DIM:(2 * g + 1) * HEAD_DIM])
        v_new = _bf16_round(new_ref[0, :, (2 * g + 1) * HEAD_DIM:(2 * g + 2) * HEAD_DIM])
        s = _dot_t(q, k) * ATTN_SCALE + neg
        s_new = jnp.sum(q.astype(F32) * k_new, axis=-1, keepdims=True) * ATTN_SCALE + new_neg
        p, p_new = _softmax_with_new(s, s_new)
        o_s = _dot(p.astype(BF16), v) + _bf16_round(p_new) * v_new

        kw = win_ref[0, :, 2 * g * HEAD_DIM:(2 * g + 1) * HEAD_DIM].astype(BF16)
        vw = win_ref[0, :, (2 * g + 1) * HEAD_DIM:(2 * g + 2) * HEAD_DIM].astype(BF16)
        sw = _dot_t(q, kw) * ATTN_SCALE
        pw = jnp.exp(sw - jnp.max(sw, axis=-1, keepdims=True))
        pw = pw * (1.0 / jnp.sum(pw, axis=-1, keepdims=True))
        o_w = _dot(pw.astype(BF16), vw)
        gt = gate[rows, :]
        o = gt[:, 0:1] * oc_ref[0, rows, :] + gt[:, 1:2] * o_s + gt[:, 2:3] * o_w
        o_ref[0, rows, :] = o.astype(o_ref.dtype)


def _dsa_sample_score_body(pt_ref, q_ref, w_ref, kn_ref, pool_hbm, o_ref, buf, sem, *, npages):
    b = pl.program_id(0)

    def page_copy(p):
        return pltpu.make_async_copy(pool_hbm.at[pt_ref[b * npages + p]], buf.at[p], sem)

    def start(p, c):
        page_copy(p).start()
        return c

    lax.fori_loop(0, npages, start, 0)
    buf[npages] = kn_ref[0]

    def wait(p, c):
        page_copy(p).wait()
        return c

    lax.fori_loop(0, npages, wait, 0)
    q = q_ref[0].astype(BF16)
    w = w_ref[0] * IDX_SCALE
    parts = [jnp.sum(jnp.maximum(_dot(q, buf[p].astype(BF16)), 0.0) * w, axis=0, keepdims=True)
             for p in range(npages + 1)]
    o_ref[0] = jnp.concatenate(parts, axis=1)


def _dsa_sample_select_body(s_ref, tri_ref, o_ref, cut_scr, rank_scr, *, length, topk):
    x = s_ref[...]
    nrow, width = x.shape
    shape = (nrow, 1)
    t = lax.broadcasted_iota(I32, (1, width), 1)
    key = _ordered_key(jnp.where(t < length, x, BELOW_NEG_INF))

    def count(flags):
        return jnp.sum(flags, axis=-1, keepdims=True)

    thr = _radix_kth(lambda cand: count(jnp.where(key >= cand, 1.0, 0.0)), shape, topk)
    n_ge = count(jnp.where(key >= thr, 1.0, 0.0))
    n_gt = count(jnp.where(key > thr, 1.0, 0.0))
    cut_scr[...] = jnp.full(shape, width, I32)

    @pl.when(jnp.max(n_ge) > topk)
    def _():
        cut_scr[...] = _tie_cutoff(
            lambda c: count(jnp.where(key == thr, jnp.where(t < c, 1.0, 0.0), 0.0)), topk - n_gt, shape,
            int(width).bit_length())

    cut = cut_scr[...]
    chosen = jnp.where(key > thr, 1.0, jnp.where(key == thr, jnp.where(t <= cut, 1.0, 0.0), 0.0))

    tri = tri_ref[...]
    ones = jnp.ones((LANES, LANES), BF16)
    before = jnp.zeros((nrow, LANES), F32)
    n_chunk = width // LANES
    for c in range(n_chunk):
        mc = chosen[:, c * LANES:(c + 1) * LANES]
        mcb = mc.astype(BF16)
        rank_scr[:, c * LANES:(c + 1) * LANES] = jnp.where(mc > 0.5, _dot(mcb, tri) + before - 1.0, -1.0)
        before = before + _dot(mcb, ones)

    wanted = lax.broadcasted_iota(I32, (topk, LANES), 0).astype(F32)
    lane = lax.broadcasted_iota(I32, (1, LANES), 1)

    def per_row(b, carry):
        ranks = rank_scr[pl.ds(b, 1), :]
        acc = jnp.zeros((topk, LANES), F32)
        for c in range(n_chunk):
            pos = (lane + c * LANES).astype(F32)
            acc = acc + jnp.where(ranks[:, c * LANES:(c + 1) * LANES] == wanted, pos, 0.0)
        o_ref[pl.ds(b, 1), :] = jnp.sum(acc.T, axis=0, keepdims=True).astype(I32)
        return carry

    lax.fori_loop(0, nrow, per_row, 0)


def _dsa_sample_attn_body(idx_ref, pt_ref, q_ref, idxv_ref, new_ref, pool_hbm, o_ref, buf, sem, *, npages, topk, past):
    b = pl.program_id(0)
    rep = GRP_DSA
    gc_rows = KV_DSA * 2

    def row_copy(r):
        t = jnp.minimum(idx_ref[b * topk + r], past - 1)
        row = (pt_ref[b * npages + _div_pow2(t, PAGE_SIZE)] * PAGE_SIZE + _mod_pow2(t, PAGE_SIZE)) * gc_rows
        return pltpu.make_async_copy(pool_hbm.at[pl.ds(pl.multiple_of(row, gc_rows), gc_rows)],
                                     buf.at[pl.ds(pl.multiple_of(r * gc_rows, gc_rows), gc_rows)], sem)

    def start(r, c):
        row_copy(r).start()
        return c

    lax.fori_loop(0, topk, start, 0)

    def wait(r, c):
        row_copy(r).wait()
        return c

    lax.fori_loop(0, topk, wait, 0)

    idxv = idxv_ref[0]
    neg = jnp.where(idxv < past, 0.0, -jnp.inf)
    new_neg = jnp.where(jnp.max(jnp.where(idxv >= past, 1.0, 0.0), axis=-1, keepdims=True) > 0.5, 0.0, -jnp.inf)
    for g in range(KV_DSA):
        rows = slice(g * rep, (g + 1) * rep)
        q = q_ref[0, rows, :]
        k = buf[pl.ds(2 * g, topk, stride=gc_rows), :].astype(BF16)
        v = buf[pl.ds(2 * g + 1, topk, stride=gc_rows), :].astype(BF16)
        k_new = _bf16_round(new_ref[0, :, 2 * g * HEAD_DIM:(2 * g + 1) * HEAD_DIM])
        v_new = _bf16_round(new_ref[0, :, (2 * g + 1) * HEAD_DIM:(2 * g + 2) * HEAD_DIM])
        s = _dot_t(q, k) * ATTN_SCALE + neg
        s_new = jnp.sum(q.astype(F32) * k_new, axis=-1, keepdims=True) * ATTN_SCALE + new_neg
        p, p_new = _softmax_with_new(s, s_new)
        o_ref[0, rows, :] = (_dot(p.astype(BF16), v) + _bf16_round(p_new) * v_new).astype(o_ref.dtype)


def _rope_tables(pos, dim):
    half = dim // 2
    inv = ROPE_THETA ** (-jnp.arange(half, dtype=F32) / half)
    ang = pos.astype(F32)[:, None] * inv[None, :]
    cos, sin = jnp.cos(ang), jnp.sin(ang)
    reps = LANES // dim
    return (jnp.tile(jnp.concatenate([cos, cos], axis=-1), (1, reps)),
            jnp.tile(jnp.concatenate([-sin, sin], axis=-1), (1, reps)))


def _project(h, w, pos, tm):
    d = h.shape[1]
    splits = (H_NSA * HEAD_DIM, KV_NSA * 2 * HEAD_DIM, KV_NSA * 2 * HEAD_DIM, KV_NSA * 2 * HEAD_DIM, 3 * H_NSA,
              H_DSA * HEAD_DIM, KV_DSA * 2 * HEAD_DIM, IDX_HEADS * IDX_DIM, IDX_DIM, IDX_HEADS, 2 * d)
    offs = np.concatenate([[0], np.cumsum(splits)])
    seg = lambda i: w[:, int(offs[i]):int(offs[i + 1])].astype(BF16)
    w_qn, w_kvc, w_kvs, w_kvw, w_gn, w_qd, w_kvd, w_qi, w_ki, w_wi, w_gm = (seg(i) for i in range(11))
    w_misc = jnp.concatenate([w_qi, w_ki, w_wi, w_gn, w_ki, w_ki], axis=1)
    assert w_misc.shape[1] == MISC_W
    cos128, sin128 = _rope_tables(pos, HEAD_DIM)
    cos64, sin64 = _rope_tables(pos, IDX_DIM)
    rope_all = lambda n: jnp.ones((1, n), F32)
    rope_k = lambda n: jnp.tile(jnp.concatenate([jnp.ones((1, HEAD_DIM), F32), jnp.zeros((1, HEAD_DIM), F32)], 1),
                                (1, n // (2 * HEAD_DIM)))

    def rope_mm(wseg, flag, dtypes, name):
        return _mm([h], [wseg], [(cos128, "row"), (sin128, "row"), (flag, "col")], _epi_rope128, dtypes,
                   tm=tm, tn=512, name=name)

    out = {}
    out["qn"], = rope_mm(w_qn, rope_all(w_qn.shape[1]), [BF16], "proj_qn")
    out["kvc"], = rope_mm(w_kvc, rope_k(w_kvc.shape[1]), [F32], "proj_kvc")
    out["kvs"], out["kvs_bf"] = rope_mm(w_kvs, rope_k(w_kvs.shape[1]), [F32, BF16], "proj_kvs")
    out["kvw"], out["kvw_bf"] = rope_mm(w_kvw, rope_k(w_kvw.shape[1]), [F32, BF16], "proj_kvw")
    out["qd"], = rope_mm(w_qd, rope_all(w_qd.shape[1]), [BF16], "proj_qd")
    out["kvd"], out["kvd_bf"] = rope_mm(w_kvd, rope_k(w_kvd.shape[1]), [F32, BF16], "proj_kvd")
    out["misc"], = _mm([h], [w_misc], [(cos64, "row"), (sin64, "row")], _epi_misc, [F32],
                       tm=min(tm, 512), tn=MISC_W, name="proj_misc")
    out["gm"], = _mm([h], [w_gm], [], _epi_sigmoid, [F32], tm=tm, tn=512, name="proj_gm")
    return out


def _merge_and_tail(x, o_n, o_d, gm, p, wts, tm):
    d = x.shape[1]
    tn = 512
    mix, = _mm([o_n, o_d], [wts["w_bn"], wts["w_bd"]], [(gm, ("tile", 0)), (gm, ("tile", d // tn))],
               _epi_merge, [BF16], tm=tm, tn=tn, name="merge")
    x1, = _mm([mix], [wts["w_out"]], [(x, ("tile", 0))], _epi_residual, [F32], tm=tm, tn=tn, name="out_proj")
    h2 = _rmsnorm(x1, wts["norm_mlp"], BF16)
    u, = _mm([h2], [wts["w_up"]], [], _epi_relu2, [BF16], tm=tm, tn=tn, name="mlp_up")
    x2, = _mm([u], [wts["w_down"]], [(x1, ("tile", 0))], _epi_residual, [F32], tm=tm, tn=1024, tk=2048,
              name="mlp_down")
    h3 = _rmsnorm(x2, wts["norm_ple"], BF16)
    x3, = _mm([h3, p], [wts["w_ple_gate"], wts["w_ple"]], [(x2, ("tile", 0))], _epi_ple, [F32],
              tm=tm, tn=tn, name="ple")
    return _rmsnorm(x3, wts["norm_final"], F32)


def kernel(x_prompt, x_sample, cache_nsa_cmp, cache_nsa_sel, cache_dsa_kv, cache_dsa_idx, state_nsa_win, page_table, p_prompt, p_sample, norm_mix, w_in, cmp_pos, cmp_w1, cmp_w2, w_branch_nsa, w_branch_dsa, w_out, norm_mlp, w_up, w_down, norm_ple, w_ple_gate, w_ple, norm_final):
    depth = w_in.shape[0]
    assert depth == 1
    bsz, seq, d = x_prompt.shape
    nb, dec_seq, _ = x_sample.shape
    assert dec_seq == 1
    npages = page_table.shape[1]
    past = npages * PAGE_SIZE
    length = past + dec_seq
    n_pool = cache_nsa_cmp.shape[1]
    gc_n = KV_NSA * 2
    gc_d = KV_DSA * 2

    wts = {
        "w_bn": w_branch_nsa[0], "w_bd": w_branch_dsa[0], "w_out": w_out[0], "w_up": w_up[0], "w_down": w_down[0],
        "w_ple_gate": w_ple_gate[0], "w_ple": w_ple[0],
        "norm_mlp": norm_mlp[0], "norm_ple": norm_ple[0], "norm_final": norm_final,
    }
    w1 = cmp_w1[0].reshape(2, 2, CMP_STRIDE * HEAD_DIM, CMP_HID).astype(BF16)
    w2 = cmp_w2[0].astype(BF16)
    pos_emb = cmp_pos[0]

    mp = bsz * seq
    xp = x_prompt.reshape(mp, d)
    hp = _rmsnorm(xp, norm_mix[0], BF16)
    pr = _project(hp, w_in[0], jnp.arange(seq), tm=1024)

    chunk_rows = CMP_STRIDE * gc_n
    pages_p = seq * gc_n // (PAGE_SIZE * gc_n)
    kvc_p = _compress(pr["kvc"].reshape(mp * gc_n, HEAD_DIM),
                      jnp.arange(bsz * pages_p, dtype=I32).reshape(bsz, pages_p), PAGE_SIZE * gc_n, pos_emb, w1, w2)
    n_q = IDX_HEADS * IDX_DIM
    gn_p = pr["misc"][:, n_q + IDX_DIM + IDX_HEADS:n_q + IDX_DIM + IDX_HEADS + 3 * H_NSA]
    o_n_p = _nsa_prompt(pr["qn"], kvc_p, pr["kvs_bf"], pr["kvw_bf"], gn_p, bsz, seq)
    o_d_p = _dsa_prompt(pr["qd"], pr["kvd_bf"], pr["misc"], bsz, seq)
    y_p = _merge_and_tail(xp, o_n_p.reshape(mp, -1), o_d_p.reshape(mp, -1), pr["gm"],
                          p_prompt[0].reshape(mp, -1), wts, tm=1024)

    xs = x_sample.reshape(nb, d)
    hs = _rmsnorm(xs, norm_mix[0], BF16)
    sm = _project(hs, w_in[0], jnp.full((nb,), past, I32), tm=nb)
    misc_s = sm["misc"]
    pt_flat = page_table.reshape(-1).astype(I32)

    kvc_s = _compress(cache_nsa_cmp[0].reshape(n_pool * PAGE_SIZE * gc_n, HEAD_DIM), page_table,
                      PAGE_SIZE * gc_n, pos_emb, w1, w2)
    nc_s = kvc_s.shape[1]
    n_sel_s = -(-length // SEL_LEN)
    n_top_s = min(SEL_TOP, n_sel_s)
    sel_lanes = -(-n_sel_s // LANES) * LANES
    ov_s = _overlap_matrix(nc_s, (length - CMP_LEN) // CMP_STRIDE + 1, n_sel_s, sel_lanes)
    qn_s = sm["qn"].reshape(nb, H_NSA, HEAD_DIM)
    oc_s, score_s = pl.pallas_call(
        functools.partial(_nsa_sample_cmp_body, qpos=length - 1, n_sel=n_sel_s),
        grid=(nb,),
        in_specs=[pl.BlockSpec((1, H_NSA, HEAD_DIM), lambda b: (b, 0, 0)),
                  pl.BlockSpec((1, nc_s, gc_n * HEAD_DIM), lambda b: (b, 0, 0)),
                  pl.BlockSpec((nc_s, sel_lanes), lambda b: (0, 0))],
        out_specs=[pl.BlockSpec((1, H_NSA, HEAD_DIM), lambda b: (b, 0, 0)),
                   pl.BlockSpec((1, 8, sel_lanes), lambda b: (b, 0, 0))],
        out_shape=[jax.ShapeDtypeStruct((nb, H_NSA, HEAD_DIM), F32),
                   jax.ShapeDtypeStruct((nb, 8, sel_lanes), F32)],
        compiler_params=_cparams(("parallel",)),
        name="nsa_sample_cmp",
    )(qn_s, kvc_s, ov_s)
    assert n_top_s <= LANES
    sel_vec = pl.pallas_call(
        functools.partial(_topk_index_body, n_top=n_top_s),
        out_shape=jax.ShapeDtypeStruct((nb * KV_NSA, LANES), I32),
        compiler_params=pltpu.CompilerParams(vmem_limit_bytes=VMEM_LIMIT),
        name="nsa_sample_topk",
    )(score_s[:, :KV_NSA].reshape(nb * KV_NSA, sel_lanes))
    sel_idx = sel_vec[:, :n_top_s].reshape(-1)

    n_past_blk = past // SEL_LEN
    win_s = jnp.concatenate([state_nsa_win[0], sm["kvw"].reshape(nb, 1, KV_NSA, 2, HEAD_DIM)], axis=1)[:, dec_seq:]
    wb = win_s.shape[1]
    gn_s = misc_s[:, n_q + IDX_DIM + IDX_HEADS:n_q + IDX_DIM + IDX_HEADS + 3 * H_NSA]
    gates_s = jnp.pad(gn_s.reshape(nb, H_NSA, 3), ((0, 0), (0, 0), (0, LANES - 3)))
    n_gather = n_top_s * SEL_LEN
    expand = jnp.asarray((np.arange(n_gather)[None, :] // SEL_LEN == np.arange(LANES)[:, None]).astype(np.float32),
                         dtype=BF16)
    o_n_s = pl.pallas_call(
        functools.partial(_nsa_sample_attn_body, npages=npages, n_top=n_top_s, n_past_blk=n_past_blk),
        grid_spec=pltpu.PrefetchScalarGridSpec(
            num_scalar_prefetch=2,
            grid=(nb,),
            in_specs=[pl.BlockSpec((1, H_NSA, HEAD_DIM), lambda b, sel, pt: (b, 0, 0)),
                      pl.BlockSpec((1, KV_NSA, LANES), lambda b, sel, pt: (b, 0, 0)),
                      pl.BlockSpec((1, 1, gc_n * HEAD_DIM), lambda b, sel, pt: (b, 0, 0)),
                      pl.BlockSpec((1, H_NSA, LANES), lambda b, sel, pt: (b, 0, 0)),
                      pl.BlockSpec((1, H_NSA, HEAD_DIM), lambda b, sel, pt: (b, 0, 0)),
                      pl.BlockSpec((1, wb, gc_n * HEAD_DIM), lambda b, sel, pt: (b, 0, 0)),
                      pl.BlockSpec((LANES, n_gather), lambda b, sel, pt: (0, 0)),
                      pl.BlockSpec(memory_space=pl.ANY)],
            out_specs=pl.BlockSpec((1, H_NSA, HEAD_DIM), lambda b, sel, pt: (b, 0, 0)),
            scratch_shapes=[pltpu.VMEM((KV_NSA * n_gather * gc_n, HEAD_DIM), F32), pltpu.SemaphoreType.DMA(())]),
        out_shape=jax.ShapeDtypeStruct((nb, H_NSA, HEAD_DIM), BF16),
        compiler_params=_cparams(("arbitrary",)),
        name="nsa_sample_attn",
    )(sel_idx, pt_flat, qn_s, sel_vec.reshape(nb, KV_NSA, LANES), sm["kvs"].reshape(nb, 1, -1), gates_s, oc_s,
      win_s.reshape(nb, wb, gc_n * HEAD_DIM), expand, cache_nsa_sel[0].reshape(n_pool * PAGE_SIZE * gc_n, HEAD_DIM))

    qi_s = misc_s[:, :n_q].reshape(nb, IDX_HEADS, IDX_DIM)
    wi_s = misc_s[:, n_q + IDX_DIM:n_q + IDX_DIM + IDX_HEADS].reshape(nb, IDX_HEADS, 1)
    ki_new = jnp.pad(misc_s[:, n_q:n_q + IDX_DIM].reshape(nb, IDX_DIM, 1), ((0, 0), (0, 0), (0, PAGE_SIZE - 1)))
    width = (npages + 1) * PAGE_SIZE
    score_s = pl.pallas_call(
        functools.partial(_dsa_sample_score_body, npages=npages),
        grid_spec=pltpu.PrefetchScalarGridSpec(
            num_scalar_prefetch=1,
            grid=(nb,),
            in_specs=[pl.BlockSpec((1, IDX_HEADS, IDX_DIM), lambda b, pt: (b, 0, 0)),
                      pl.BlockSpec((1, IDX_HEADS, 1), lambda b, pt: (b, 0, 0)),
                      pl.BlockSpec((1, IDX_DIM, PAGE_SIZE), lambda b, pt: (b, 0, 0)),
                      pl.BlockSpec(memory_space=pl.ANY)],
            out_specs=pl.BlockSpec((1, 1, width), lambda b, pt: (b, 0, 0)),
            scratch_shapes=[pltpu.VMEM((npages + 1, IDX_DIM, PAGE_SIZE), F32), pltpu.SemaphoreType.DMA(())]),
        out_shape=jax.ShapeDtypeStruct((nb, 1, width), F32),
        compiler_params=_cparams(("arbitrary",)),
        name="dsa_sample_score",
    )(pt_flat, qi_s, wi_s, ki_new, jnp.swapaxes(cache_dsa_idx[0], 1, 2))
    topk_s = min(DSA_TOPK_MAX, length // 4)
    tri = jnp.asarray(np.triu(np.ones((LANES, LANES), np.float32)), dtype=BF16)
    top_idx = pl.pallas_call(
        functools.partial(_dsa_sample_select_body, length=length, topk=topk_s),
        out_shape=jax.ShapeDtypeStruct((nb, topk_s), I32),
        scratch_shapes=[pltpu.VMEM((nb, 1), I32), pltpu.VMEM((nb, width), F32)],
        compiler_params=pltpu.CompilerParams(vmem_limit_bytes=VMEM_LIMIT),
        name="dsa_sample_select",
    )(score_s.reshape(nb, width), tri)
    o_d_s = pl.pallas_call(
        functools.partial(_dsa_sample_attn_body, npages=npages, topk=topk_s, past=past),
        grid_spec=pltpu.PrefetchScalarGridSpec(
            num_scalar_prefetch=2,
            grid=(nb,),
            in_specs=[pl.BlockSpec((1, H_DSA, HEAD_DIM), lambda b, ix, pt: (b, 0, 0)),
                      pl.BlockSpec((1, 1, topk_s), lambda b, ix, pt: (b, 0, 0)),
                      pl.BlockSpec((1, 1, gc_d * HEAD_DIM), lambda b, ix, pt: (b, 0, 0)),
                      pl.BlockSpec(memory_space=pl.ANY)],
            out_specs=pl.BlockSpec((1, H_DSA, HEAD_DIM), lambda b, ix, pt: (b, 0, 0)),
            scratch_shapes=[pltpu.VMEM((topk_s * gc_d, HEAD_DIM), F32), pltpu.SemaphoreType.DMA(())]),
        out_shape=jax.ShapeDtypeStruct((nb, H_DSA, HEAD_DIM), BF16),
        compiler_params=_cparams(("arbitrary",)),
        name="dsa_sample_attn",
    )(top_idx.reshape(-1), pt_flat, sm["qd"].reshape(nb, H_DSA, HEAD_DIM), top_idx.reshape(nb, 1, topk_s),
      sm["kvd"].reshape(nb, 1, -1), cache_dsa_kv[0].reshape(n_pool * PAGE_SIZE * gc_d, HEAD_DIM))

    y_s = _merge_and_tail(xs, o_n_s.reshape(nb, -1), o_d_s.reshape(nb, -1), sm["gm"],
                          p_sample[0].reshape(nb, -1), wts, tm=nb)

    kv6 = lambda a, n, s, g: a.reshape(1, n, s, g, 2, HEAD_DIM)
    n_q_end = n_q + IDX_DIM
    return (y_p.reshape(bsz, seq, d), y_s.reshape(nb, dec_seq, d),
            kv6(pr["kvc"], bsz, seq, KV_NSA), kv6(sm["kvc"], nb, dec_seq, KV_NSA),
            kv6(pr["kvs"], bsz, seq, KV_NSA), kv6(sm["kvs"], nb, dec_seq, KV_NSA),
            kv6(pr["kvd"], bsz, seq, KV_DSA), kv6(sm["kvd"], nb, dec_seq, KV_DSA),
            pr["misc"][:, n_q:n_q_end].reshape(1, bsz, seq, IDX_DIM),
            misc_s[:, n_q:n_q_end].reshape(1, nb, dec_seq, IDX_DIM),
            kv6(pr["kvw"], bsz, seq, KV_NSA)[:, :, seq - min(WINDOW, seq):],
            win_s[None])
```
